```python
import math
import jax, jax.numpy as jnp
from jax import lax
import numpy as np

D_MODEL = 2048
BATCH = 2
SEQ = 4096
DEPTH = 1

SGU_WIDTH = D_MODEL
SGU_CHUNK = 128
SGU_GROUPS = 16
SGU_GROUP_DIM = SGU_WIDTH // SGU_GROUPS
RWKV_HEAD = 64
RWKV_WIDTH = D_MODEL
RWKV_HEADS = RWKV_WIDTH // RWKV_HEAD
DECAY_LORA = 96
ICLR_LORA = 96
GATE_LORA = 256
LNX_EPS = 64e-5
N_EXPERTS = 256
TOP_K = 8
N_GROUPS = 8
TOPK_GROUPS = 4
EXPERT_FF = 512
SHARED_FF = 512
ROUTED_SCALE = 2.5
EXPERT_BLOCK = 128
PLE_DIM = 256
LN_EPS = 1e-5
DEEPNORM_ALPHA = (2 * DEPTH) ** 0.25
DEEPNORM_BETA = (8 * DEPTH) ** -0.25
RWKV_COLS = 3 * RWKV_WIDTH + DECAY_LORA + ICLR_LORA + GATE_LORA
IN_COLS = 2 * SGU_WIDTH + RWKV_COLS + 2 * D_MODEL

kernel_name = "hybrid_sgu_rwkv7_moe_deepnorm"


def layer_norm(x, g, b, eps=LN_EPS):
    xf = x.astype(jnp.float32)
    mu = jnp.mean(xf, axis=-1, keepdims=True)
    var = jnp.mean(jnp.square(xf - mu), axis=-1, keepdims=True)
    y = (xf - mu) * lax.rsqrt(var + eps) * g.astype(jnp.float32) + b.astype(jnp.float32)
    return y.astype(x.dtype)


def chunked_sgu(z, ln_g, ln_b, w_s, b_s):
    u, v = jnp.split(z, 2, axis=-1)
    v = layer_norm(v, ln_g, ln_b)
    bsz, seq, width = v.shape
    v = v.reshape(bsz, seq // SGU_CHUNK, SGU_CHUNK, SGU_GROUPS, SGU_GROUP_DIM)
    causal = jnp.tril(jnp.ones((SGU_CHUNK, SGU_CHUNK), dtype=bool))
    w = jnp.where(causal[None], w_s, 0.0).astype(v.dtype)
    vm = jnp.einsum('gij,bcjgd->bcigd', w, v) + b_s.T[None, None, :, :, None].astype(v.dtype)
    return u * vm.reshape(bsz, seq, width)


def rwkv7_time_mix(proj, mu, w0, w2, a0, a2, g2, k_k, k_a, r_k, lnx_g, lnx_b):
    f32 = jnp.float32
    prev = jnp.pad(proj[:, :-1], ((0, 0), (1, 0), (0, 0)))
    q = proj + (prev - proj) * mu
    split_at = [RWKV_WIDTH, 2 * RWKV_WIDTH, 3 * RWKV_WIDTH,
                3 * RWKV_WIDTH + DECAY_LORA, 3 * RWKV_WIDTH + DECAY_LORA + ICLR_LORA]
    r, k, v, zw, za, zg = jnp.split(q, split_at, axis=-1)
    w_log = -jax.nn.softplus(-(w0 + jnp.tanh(zw) @ w2).astype(f32)) - 0.5
    decay = jnp.exp(-jnp.exp(w_log))
    a = jax.nn.sigmoid((a0 + za @ a2).astype(f32))
    g = jax.nn.sigmoid(zg) @ g2
    bsz, seq, _ = r.shape

    def heads(t):
        return t.astype(f32).reshape(bsz, seq, RWKV_HEADS, RWKV_HEAD)

    r_h, k_h, v_h, w_h, a_h = heads(r), heads(k), heads(v), heads(decay), heads(a)
    kk = heads(k * k_k)
    kk = kk / jnp.maximum(jnp.sqrt(jnp.sum(kk * kk, axis=-1, keepdims=True)), 1e-12)
    k_h = k_h * (1.0 + (a_h - 1.0) * k_a.astype(f32).reshape(RWKV_HEADS, RWKV_HEAD))

    def step(state, inp):
        r_t, w_t, k_t, v_t, ka_t, kb_t = inp
        sa = jnp.einsum('bhij,bhj->bhi', state, ka_t)
        state = (state * w_t[:, :, None, :] + sa[..., None] * kb_t[:, :, None, :]
                 + v_t[..., None] * k_t[:, :, None, :])
        y_t = jnp.einsum('bhij,bhj->bhi', state, r_t)
        return state, y_t

    seq_first = lambda t: jnp.moveaxis(t, 1, 0)
    state0 = jnp.zeros((bsz, RWKV_HEADS, RWKV_HEAD, RWKV_HEAD), f32)
    xs = (seq_first(r_h), seq_first(w_h), seq_first(k_h), seq_first(v_h),
          seq_first(-kk), seq_first(kk * a_h))
    _, y = lax.scan(step, state0, xs)
    y = jnp.moveaxis(y, 0, 1)
    ym = jnp.mean(y, axis=-1, keepdims=True)
    yv = jnp.mean(jnp.square(y - ym), axis=-1, keepdims=True)
    y = ((y - ym) * lax.rsqrt(yv + LNX_EPS)).reshape(bsz, seq, RWKV_WIDTH)
    y = y * lnx_g.astype(f32) + lnx_b.astype(f32)
    bonus = jnp.sum(r_h * k_h * r_k.astype(f32), axis=-1, keepdims=True) * v_h
    y = y + bonus.reshape(bsz, seq, RWKV_WIDTH)
    return (y * g.astype(f32)).astype(proj.dtype)


def route(xt, w_r, bias):
    f32 = jnp.float32
    scores = jax.nn.sigmoid((xt @ w_r).astype(f32))
    biased = scores + bias.astype(f32)
    t = scores.shape[0]
    grp = biased.reshape(t, N_GROUPS, N_EXPERTS // N_GROUPS)
    grp_score = jnp.sum(lax.top_k(grp, 2)[0], axis=-1)
    _, grp_idx = lax.top_k(grp_score, TOPK_GROUPS)
    grp_mask = jnp.any(grp_idx[..., None] == jnp.arange(N_GROUPS), axis=-2)
    elem_mask = jnp.repeat(grp_mask, N_EXPERTS // N_GROUPS, axis=-1)
    _, idx = lax.top_k(jnp.where(elem_mask, biased, -jnp.inf), TOP_K)
    w = jnp.take_along_axis(scores, idx, axis=-1)
    w = w / jnp.sum(w, axis=-1, keepdims=True) * ROUTED_SCALE
    return idx, w


def routed_experts(xt, idx, wts, w_gate, w_up, w_down, layer):
    t, d = xt.shape
    n_assign = t * TOP_K
    n_blocks = (n_assign + N_EXPERTS * (EXPERT_BLOCK - 1) + EXPERT_BLOCK - 1) // EXPERT_BLOCK
    n_rows = n_blocks * EXPERT_BLOCK
    flat_e = idx.reshape(-1)
    flat_w = wts.reshape(-1)
    order = jnp.argsort(flat_e)
    sorted_e = flat_e[order]
    sorted_tok = order // TOP_K
    counts = jnp.bincount(flat_e, length=N_EXPERTS)
    start = jnp.cumsum(counts) - counts
    padded = (counts + EXPERT_BLOCK - 1) // EXPERT_BLOCK * EXPERT_BLOCK
    pad_end = jnp.cumsum(padded)
    pad_start = pad_end - padded
    dest = pad_start[sorted_e] + (jnp.arange(n_assign) - start[sorted_e])
    tok_pad = jnp.full((n_rows,), t, dtype=jnp.int32).at[dest].set(sorted_tok.astype(jnp.int32))
    w_pad = jnp.zeros((n_rows,), xt.dtype).at[dest].set(flat_w[order].astype(xt.dtype))
    blk_e = jnp.minimum(jnp.searchsorted(pad_end, jnp.arange(n_blocks) * EXPERT_BLOCK, side='right'),
                        N_EXPERTS - 1)
    x_ext = jnp.concatenate([xt, jnp.zeros((1, d), xt.dtype)], axis=0)

    def one_block(args):
        tok, e = args
        xb = x_ext[tok]
        h = jax.nn.silu(xb @ w_gate[layer, e]) * (xb @ w_up[layer, e])
        return h @ w_down[layer, e]

    y = lax.map(one_block, (tok_pad.reshape(n_blocks, EXPERT_BLOCK), blk_e))
    y = y.reshape(n_rows, d) * w_pad[:, None]
    return jnp.zeros((t + 1, d), xt.dtype).at[tok_pad].add(y)[:t]


def swiglu(x, w_g, w_u, w_d):
    return (jax.nn.silu(x @ w_g) * (x @ w_u)) @ w_d


def setup_inputs(seed: int = 0) -> dict:
    key = jax.random.key(seed)
    ks = iter(jax.random.split(key, 40))
    f32 = jnp.float32
    L = DEPTH

    def nrm(shape, scale):
        return jax.random.normal(next(ks), shape, f32) * scale

    return {
        "x": nrm((BATCH, SEQ, D_MODEL), 1.0),
        "p": nrm((DEPTH, BATCH, SEQ, PLE_DIM), 1.0),
        "w_in": nrm((L, D_MODEL, IN_COLS), D_MODEL ** -0.5),
        "mu_shift": jax.random.uniform(next(ks), (L, RWKV_COLS), f32),
        "w0": jax.random.uniform(next(ks), (L, RWKV_WIDTH), f32, -6.5, -1.5),
        "w2": nrm((L, DECAY_LORA, RWKV_WIDTH), DECAY_LORA ** -0.5),
        "a0": nrm((L, RWKV_WIDTH), 0.1),
        "a2": nrm((L, ICLR_LORA, RWKV_WIDTH), ICLR_LORA ** -0.5),
        "g2": nrm((L, GATE_LORA, RWKV_WIDTH), GATE_LORA ** -0.5),
        "k_k": 0.85 + nrm((L, RWKV_WIDTH), 0.05),
        "k_a": 1.0 + nrm((L, RWKV_WIDTH), 0.05),
        "r_k": nrm((L, RWKV_HEADS, RWKV_HEAD), 0.1),
        "lnx_g": 1.0 + nrm((L, RWKV_WIDTH), 0.02),
        "lnx_b": nrm((L, RWKV_WIDTH), 0.02),
        "sgu_ln_g": 1.0 + nrm((L, SGU_WIDTH), 0.02),
        "sgu_ln_b": nrm((L, SGU_WIDTH), 0.02),
        "sgu_w": nrm((L, SGU_GROUPS, SGU_CHUNK, SGU_CHUNK), SGU_CHUNK ** -0.5),
        "sgu_b": 1.0 + nrm((L, SGU_GROUPS, SGU_CHUNK), 0.1),
        "w_out": nrm((L, SGU_WIDTH + RWKV_WIDTH, D_MODEL), SGU_WIDTH ** -0.5 * DEEPNORM_BETA),
        "ln1_g": 1.0 + nrm((L, D_MODEL), 0.02),
        "ln1_b": nrm((L, D_MODEL), 0.02),
        "router_w": nrm((L, D_MODEL, N_EXPERTS), D_MODEL ** -0.5),
        "router_bias": nrm((L, N_EXPERTS), 0.01),
        "exp_gate": nrm((L, N_EXPERTS, D_MODEL, EXPERT_FF), D_MODEL ** -0.5),
        "exp_up": nrm((L, N_EXPERTS, D_MODEL, EXPERT_FF), D_MODEL ** -0.5),
        "exp_down": nrm((L, N_EXPERTS, EXPERT_FF, D_MODEL), EXPERT_FF ** -0.5 * DEEPNORM_BETA),
        "sh_gate": nrm((L, D_MODEL, SHARED_FF), D_MODEL ** -0.5),
        "sh_up": nrm((L, D_MODEL, SHARED_FF), D_MODEL ** -0.5),
        "sh_down": nrm((L, SHARED_FF, D_MODEL), SHARED_FF ** -0.5 * DEEPNORM_BETA),
        "ple_gate_w": nrm((L, D_MODEL, D_MODEL), D_MODEL ** -0.5),
        "ple_proj": nrm((L, PLE_DIM, D_MODEL), PLE_DIM ** -0.5 * DEEPNORM_BETA),
        "ln2_g": 1.0 + nrm((L, D_MODEL), 0.02),
        "ln2_b": nrm((L, D_MODEL), 0.02),
    }


def reference(x, p, w_in, mu_shift, w0, w2, a0, a2, g2, k_k, k_a, r_k, lnx_g, lnx_b,
              sgu_ln_g, sgu_ln_b, sgu_w, sgu_b, w_out, ln1_g, ln1_b, router_w, router_bias,
              exp_gate, exp_up, exp_down, sh_gate, sh_up, sh_down, ple_gate_w, ple_proj,
              ln2_g, ln2_b):
    for i in range(DEPTH):
        h = x @ w_in[i]
        z_a, p_b, g_br = jnp.split(h, [2 * SGU_WIDTH, 2 * SGU_WIDTH + RWKV_COLS], axis=-1)
        out_a = chunked_sgu(jax.nn.gelu(z_a, approximate=False),
                            sgu_ln_g[i], sgu_ln_b[i], sgu_w[i], sgu_b[i])
        out_b = rwkv7_time_mix(p_b, mu_shift[i], w0[i], w2[i], a0[i], a2[i], g2[i],
                               k_k[i], k_a[i], r_k[i], lnx_g[i], lnx_b[i])
        gate_a, gate_b = jnp.split(jax.nn.sigmoid(g_br), 2, axis=-1)
        mix = (gate_a * (out_a @ w_out[i, :SGU_WIDTH])
               + gate_b * (out_b @ w_out[i, SGU_WIDTH:]))
        x = layer_norm(DEEPNORM_ALPHA * x + mix, ln1_g[i], ln1_b[i])
        xt = x.reshape(-1, D_MODEL)
        idx, wts = route(xt, router_w[i], router_bias[i])
        routed = routed_experts(xt, idx, wts, exp_gate, exp_up, exp_down, i)
        shared = swiglu(xt, sh_gate[i], sh_up[i], sh_down[i])
        ffn = (routed + shared).reshape(x.shape)
        ple = jax.nn.sigmoid(x @ ple_gate_w[i]) * (p[i] @ ple_proj[i])
        x = layer_norm(DEEPNORM_ALPHA * x + ffn + ple, ln2_g[i], ln2_b[i])
    return x
```

```python
import functools
import math

import jax
import jax.numpy as jnp
from jax import lax
from jax.experimental import pallas as pl
from jax.experimental.pallas import tpu as pltpu

F32 = jnp.float32
BF16 = jnp.bfloat16

LANES = 128
VMEM_LIMIT = 56 * 1024 * 1024

SGU_CHUNK = 128
SGU_GROUPS = 16
HEAD = 64
RWKV_CHUNK = 64
HEADS_PER_STEP = 8
DECAY_LORA = 96
ICLR_LORA = 96
GATE_LORA = 256
LNX_EPS = 64e-5
LN_EPS = 1e-5
TOP_K = 8
N_GROUPS = 8
TOPK_GROUPS = 4
ROUTED_SCALE = 2.5
EXPERT_BLOCK = 128
INV_SQRT2 = 1.0 / math.sqrt(2.0)


def _cparams(sem, vmem=VMEM_LIMIT):
    return pltpu.CompilerParams(dimension_semantics=sem, vmem_limit_bytes=vmem)


def _dot(a, b):
    return jnp.dot(a, b, preferred_element_type=F32)


def _dot_nt(a, b):
    return lax.dot_general(a, b, (((1,), (1,)), ((), ())), preferred_element_type=F32)


def _layer_norm(z, g, b, eps):
    mu = jnp.mean(z, axis=-1, keepdims=True)
    zc = z - mu
    var = jnp.mean(zc * zc, axis=-1, keepdims=True)
    return zc * lax.rsqrt(var + eps) * g + b


def _mm_act_kernel(x_ref, w_ref, o_ref, *, act):
    acc = _dot(x_ref[...], w_ref[...])
    if act == "gelu":
        acc = 0.5 * acc * (1.0 + lax.erf(acc * INV_SQRT2))
    elif act == "sigmoid":
        acc = jax.nn.sigmoid(acc)
    o_ref[...] = acc.astype(o_ref.dtype)


def _mm_act(x, w, act, out_dtype, tm, tn):
    m, k = x.shape
    n = w.shape[1]
    tm = min(tm, m)
    return pl.pallas_call(
        functools.partial(_mm_act_kernel, act=act),
        grid=(m // tm, n // tn),
        in_specs=[pl.BlockSpec((tm, k), lambda i, j: (i, 0)),
                  pl.BlockSpec((k, tn), lambda i, j: (0, j))],
        out_specs=pl.BlockSpec((tm, tn), lambda i, j: (i, j)),
        out_shape=jax.ShapeDtypeStruct((m, n), out_dtype),
        compiler_params=_cparams(("parallel", "parallel")),
        name="in_proj_" + str(act),
    )(x, w)


def _sgu_kernel(u_ref, v_ref, g_ref, b_ref, w_ref, bias_ref, o_ref):
    v = v_ref[...].astype(F32)
    vn = _layer_norm(v, g_ref[...], b_ref[...], LN_EPS).astype(BF16)
    ri = lax.broadcasted_iota(jnp.int32, (SGU_CHUNK, SGU_CHUNK), 0)
    ci = lax.broadcasted_iota(jnp.int32, (SGU_CHUNK, SGU_CHUNK), 1)
    causal = ri >= ci
    for g in range(SGU_GROUPS):
        sl = slice(g * LANES, (g + 1) * LANES)
        wg = jnp.where(causal, w_ref[g], 0.0).astype(BF16)
        vm = _dot(wg, vn[:, sl]) + bias_ref[:, sl]
        o_ref[:, sl] = (u_ref[:, sl].astype(F32) * vm).astype(o_ref.dtype)


def _sgu(za, ln_g, ln_b, w_s, bias_full):
    t = za.shape[0]
    width = za.shape[1] // 2
    return pl.pallas_call(
        _sgu_kernel,
        grid=(t // SGU_CHUNK,),
        in_specs=[pl.BlockSpec((SGU_CHUNK, width), lambda c: (c, 0)),
                  pl.BlockSpec((SGU_CHUNK, width), lambda c: (c, 1)),
                  pl.BlockSpec((1, width), lambda c: (0, 0)),
                  pl.BlockSpec((1, width), lambda c: (0, 0)),
                  pl.BlockSpec((SGU_GROUPS, SGU_CHUNK, SGU_CHUNK), lambda c: (0, 0, 0)),
                  pl.BlockSpec((SGU_CHUNK, width), lambda c: (0, 0))],
        out_specs=pl.BlockSpec((SGU_CHUNK, width), lambda c: (c, 0)),
        out_shape=jax.ShapeDtypeStruct((t, width), BF16),
        compiler_params=_cparams(("parallel",)),
        name="sgu",
    )(za, za, ln_g, ln_b, w_s, bias_full)


def _rwkv_prep_kernel(p_ref, prev_ref, mu_ref, w0_ref, w2_ref, a0_ref, a2_ref, g2_ref,
                      r_ref, k_ref, v_ref, lw_ref, a_ref, g_ref, *, tiles_per_seq, width):
    tm = p_ref.shape[0]
    first = (pl.program_id(0) % tiles_per_seq) == 0
    row0 = lax.broadcasted_iota(jnp.int32, (tm, 1), 0) == 0

    def shifted(lo, hi):
        p = p_ref[:, lo:hi]
        last = jnp.where(first, 0.0, prev_ref[7:8, lo:hi])
        prev = jnp.where(row0, last, pltpu.roll(p, 1, axis=0))
        return p + (prev - p) * mu_ref[:, lo:hi]

    r_ref[...] = shifted(0, width)
    k_ref[...] = shifted(width, 2 * width)
    v_ref[...] = shifted(2 * width, 3 * width)
    o = 3 * width
    zw = shifted(o, o + LANES)
    za = shifted(o + LANES, o + 2 * LANES)
    zg = shifted(o + 2 * LANES, o + 2 * LANES + GATE_LORA)
    dw = w0_ref[...] + _dot(jnp.tanh(zw).astype(BF16), w2_ref[...])
    w_log = -(jnp.maximum(-dw, 0.0) + jnp.log(1.0 + jnp.exp(-jnp.abs(dw)))) - 0.5
    lw_ref[...] = -jnp.exp(w_log)
    a_ref[...] = jax.nn.sigmoid(a0_ref[...] + _dot(za.astype(BF16), a2_ref[...]))
    g_ref[...] = _dot(jax.nn.sigmoid(zg).astype(BF16), g2_ref[...])


def _rwkv_prep(proj, mu, w0, w2p, a0, a2p, g2, seq, tm=128):
    t, pc = proj.shape
    width = w0.shape[1]
    row = lambda i: (i, 0)
    const = lambda i: (0, 0)
    out = jax.ShapeDtypeStruct((t, width), F32)
    return pl.pallas_call(
        functools.partial(_rwkv_prep_kernel, tiles_per_seq=seq // tm, width=width),
        grid=(t // tm,),
        in_specs=[pl.BlockSpec((tm, pc), row),
                  pl.BlockSpec((8, pc), lambda i: (jnp.maximum(i * (tm // 8) - 1, 0), 0)),
                  pl.BlockSpec((1, pc), const),
                  pl.BlockSpec((1, width), const),
                  pl.BlockSpec((LANES, width), const),
                  pl.BlockSpec((1, width), const),
                  pl.BlockSpec((LANES, width), const),
                  pl.BlockSpec((GATE_LORA, width), const)],
        out_specs=[pl.BlockSpec((tm, width), row)] * 6,
        out_shape=[out] * 6,
        compiler_params=_cparams(("parallel",)),
        name="rwkv_prep",
    )(proj, proj, mu, w0, w2p, a0, a2p, g2)


def _rwkv_chunk_kernel(r_ref, k_ref, v_ref, lw_ref, a_ref, g_ref,
                       kk_ref, ka_ref, rk_ref, lg_ref, lb_ref,
                       o_ref, h_scr, y_scr):
    c = RWKV_CHUNK
    w = r_ref.shape[1]
    nh = w // HEAD

    @pl.when(pl.program_id(2) == 0)
    def _():
        h_scr[...] = jnp.zeros_like(h_scr)

    r = r_ref[...]
    k = k_ref[...]
    v = v_ref[...]
    lw = lw_ref[...]
    a = a_ref[...]

    ti = lax.broadcasted_iota(jnp.int32, (c, c), 0)
    tj = lax.broadcasted_iota(jnp.int32, (c, c), 1)
    strict = ti > tj
    incl = ti >= tj
    eye = (ti == tj)
    eye_f = eye.astype(F32)
    eye_b = eye.astype(BF16)

    gcum = jnp.dot(incl.astype(F32), lw, precision=lax.Precision.HIGHEST,
                   preferred_element_type=F32)
    g_last = gcum[c - 1:c, :]
    e_g = jnp.exp(gcum)
    e_ng = jnp.exp(-gcum)
    e_gp = jnp.exp(gcum - lw)
    e_h = jnp.exp(g_last - gcum)
    e_last = jnp.exp(g_last)

    li = lax.broadcasted_iota(jnp.int32, (w, w), 0) // HEAD
    lj = lax.broadcasted_iota(jnp.int32, (w, w), 1) // HEAD
    same_head = (li == lj).astype(BF16)

    def head_sum(x):
        hi = x.astype(BF16)
        lo = (x - hi.astype(F32)).astype(BF16)
        return _dot(hi, same_head) + _dot(lo, same_head)

    kk = k * kk_ref[...]
    kk = kk / jnp.maximum(jnp.sqrt(head_sum(kk * kk)), 1e-12)
    kmod = k * (1.0 + (a - 1.0) * ka_ref[...])
    kb = kk * a
    bonus = head_sum(r * kmod * rk_ref[...]) * v

    at_all = (-kk * e_gp).astype(BF16)
    bt_all = (kb * e_ng).astype(BF16)
    kt_all = (kmod * e_ng).astype(BF16)
    rt_all = r * e_g
    bh_all = (kb * e_h).astype(BF16)
    kh_all = (kmod * e_h).astype(BF16)
    v_all = v.astype(BF16)

    for h in range(nh):
        sl = slice(h * HEAD, (h + 1) * HEAD)
        at, bt, kt, vb = at_all[:, sl], bt_all[:, sl], kt_all[:, sl], v_all[:, sl]
        rt = rt_all[:, sl]
        rtb = rt.astype(BF16)
        a_ab = jnp.where(strict, _dot_nt(at, bt), 0.0)
        a_ak = jnp.where(strict, _dot_nt(at, kt), 0.0).astype(BF16)
        m_rb = jnp.where(incl, _dot_nt(rtb, bt), 0.0).astype(BF16)
        m_rk = jnp.where(incl, _dot_nt(rtb, kt), 0.0).astype(BF16)
        tinv = eye_f + a_ab
        pw = a_ab
        for _ in range(5):
            pwb = pw.astype(BF16)
            pw = _dot(pwb, pwb)
            tinv = tinv + _dot(tinv.astype(BF16), pw.astype(BF16))
        tb = tinv.astype(BF16)
        w1 = _dot(tb, at).astype(BF16)
        x2 = _dot(tb, _dot(a_ak, vb).astype(BF16)).astype(BF16)
        q1 = (rt + _dot(m_rb, w1)).astype(BF16)
        o2 = _dot(m_rb, x2) + _dot(m_rk, vb)
        bh_t = _dot_nt(eye_b, bh_all[:, sl]).astype(BF16)
        kh_t = _dot_nt(eye_b, kh_all[:, sl]).astype(BF16)
        gmat = jnp.where(eye, e_last[:, sl], 0.0) + _dot(bh_t, w1)
        dmat = _dot(bh_t, x2) + _dot(kh_t, vb)
        hb = h_scr[h].astype(BF16)
        y_scr[:, sl] = _dot(q1, hb) + o2
        h_scr[h] = _dot(gmat.astype(BF16), hb) + dmat

    y = y_scr[...]
    ym = head_sum(y) * (1.0 / HEAD)
    yc = y - ym
    yv = head_sum(yc * yc) * (1.0 / HEAD)
    yn = yc * lax.rsqrt(yv + LNX_EPS) * lg_ref[...] + lb_ref[...]
    o_ref[...] = ((yn + bonus) * g_ref[...]).astype(o_ref.dtype)


def _rwkv_chunks(r, k, v, lw, a, g, k_k, k_a, r_k, lnx_g, lnx_b, batch, seq):
    t, width = r.shape
    wblk = HEADS_PER_STEP * HEAD
    nc = seq // RWKV_CHUNK
    tok = pl.BlockSpec((RWKV_CHUNK, wblk), lambda b, hg, c: (b * nc + c, hg))
    par = pl.BlockSpec((1, wblk), lambda b, hg, c: (0, hg))
    return pl.pallas_call(
        _rwkv_chunk_kernel,
        grid=(batch, width // wblk, nc),
        in_specs=[tok] * 6 + [par] * 5,
        out_specs=tok,
        out_shape=jax.ShapeDtypeStruct((t, width), BF16),
        scratch_shapes=[pltpu.VMEM((HEADS_PER_STEP, HEAD, HEAD), F32),
                        pltpu.VMEM((RWKV_CHUNK, wblk), F32)],
        compiler_params=_cparams(("parallel", "parallel", "arbitrary")),
        name="rwkv_chunks",
    )(r, k, v, lw, a, g, k_k, k_a, r_k, lnx_g, lnx_b)


def _mix_ln_kernel(oa_ref, ob_ref, wa_ref, wb_ref, ga_ref, gb_ref, x_ref, g_ref, b_ref,
                   x1_ref, *, alpha):
    ma = _dot(oa_ref[...], wa_ref[...])
    mb = _dot(ob_ref[...], wb_ref[...])
    mix = ga_ref[...].astype(F32) * ma + gb_ref[...].astype(F32) * mb
    z = alpha * x_ref[...] + mix
    x1_ref[...] = _layer_norm(z, g_ref[...], b_ref[...], LN_EPS)


def _mix_ln(out_a, out_b, wa, wb, gates, x, ln_g, ln_b, alpha, tm=256):
    t, d = x.shape
    row = lambda i: (i, 0)
    const = lambda i: (0, 0)
    return pl.pallas_call(
        functools.partial(_mix_ln_kernel, alpha=alpha),
        grid=(t // tm,),
        in_specs=[pl.BlockSpec((tm, d), row), pl.BlockSpec((tm, d), row),
                  pl.BlockSpec((d, d), const, pipeline_mode=pl.Buffered(1)),
                  pl.BlockSpec((d, d), const, pipeline_mode=pl.Buffered(1)),
                  pl.BlockSpec((tm, d), row), pl.BlockSpec((tm, d), lambda i: (i, 1)),
                  pl.BlockSpec((tm, d), row),
                  pl.BlockSpec((1, d), const), pl.BlockSpec((1, d), const)],
        out_specs=pl.BlockSpec((tm, d), row),
        out_shape=jax.ShapeDtypeStruct((t, d), F32),
        compiler_params=_cparams(("parallel",)),
        name="mix_ln",
    )(out_a, out_b, wa, wb, gates, gates, x, ln_g, ln_b)


def _router_kernel(x_ref, wr_ref, bias_ref, idx_ref, wt_ref):
    n_exp = wr_ref.shape[0]
    tm = x_ref.shape[0]
    gsz = n_exp // N_GROUPS
    scores = jax.nn.sigmoid(_dot_nt(wr_ref[...], x_ref[...].astype(BF16)))
    biased = scores + bias_ref[...]
    neg = -jnp.inf

    grp_scores = []
    rows_g = lax.broadcasted_iota(jnp.int32, (gsz, tm), 0)
    for g in range(N_GROUPS):
        blk = biased[g * gsz:(g + 1) * gsz]
        m1 = jnp.max(blk, axis=0, keepdims=True)
        i1 = jnp.min(jnp.where(blk == m1, rows_g, gsz), axis=0, keepdims=True)
        m2 = jnp.max(jnp.where(rows_g == i1, neg, blk), axis=0, keepdims=True)
        grp_scores.append(m1 + m2)
    masked = []
    for g in range(N_GROUPS):
        rank = jnp.zeros((1, tm), jnp.int32)
        for g2 in range(N_GROUPS):
            if g2 == g:
                continue
            ahead = grp_scores[g2] > grp_scores[g]
            if g2 < g:
                ahead = ahead | (grp_scores[g2] == grp_scores[g])
            rank = rank + ahead.astype(jnp.int32)
        blk = biased[g * gsz:(g + 1) * gsz]
        masked.append(jnp.where(rank < TOPK_GROUPS, blk, neg))
    cand = jnp.concatenate(masked, axis=0)

    rows = lax.broadcasted_iota(jnp.int32, (n_exp, tm), 0)
    ids, wts = [], []
    for _ in range(TOP_K):
        m = jnp.max(cand, axis=0, keepdims=True)
        i = jnp.min(jnp.where(cand == m, rows, n_exp), axis=0, keepdims=True)
        hit = rows == i
        ids.append(i)
        wts.append(jnp.sum(jnp.where(hit, scores, 0.0), axis=0, keepdims=True))
        cand = jnp.where(hit, neg, cand)
    wt = jnp.concatenate(wts, axis=0)
    wt = wt / jnp.sum(wt, axis=0, keepdims=True) * ROUTED_SCALE
    idx_ref[...] = jnp.concatenate(ids, axis=0)
    wt_ref[...] = wt


def _router(x1, wr_t, bias_col, tm=512):
    t, d = x1.shape
    n_exp = wr_t.shape[0]
    return pl.pallas_call(
        _router_kernel,
        grid=(t // tm,),
        in_specs=[pl.BlockSpec((tm, d), lambda i: (i, 0)),
                  pl.BlockSpec((n_exp, d), lambda i: (0, 0)),
                  pl.BlockSpec((n_exp, 1), lambda i: (0, 0))],
        out_specs=[pl.BlockSpec((TOP_K, tm), lambda i: (0, i)),
                   pl.BlockSpec((TOP_K, tm), lambda i: (0, i))],
        out_shape=[jax.ShapeDtypeStruct((TOP_K, t), jnp.int32),
                   jax.ShapeDtypeStruct((TOP_K, t), F32)],
        compiler_params=_cparams(("parallel",)),
        name="router",
    )(x1, wr_t, bias_col)


def _experts_kernel(tok_ref, blk_e_ref, nused_ref, x_hbm, wg_ref, wu_ref, wd_ref, y_ref,
                    xbuf, sem, wg_b, wu_b, wd_b):
    i = pl.program_id(0)
    nused = nused_ref[0]
    slot = i % 2

    def row_copy(blk, s, r):
        t = tok_ref[blk * EXPERT_BLOCK + r]
        return pltpu.make_async_copy(x_hbm.at[pl.ds(t, 1)], xbuf.at[s, pl.ds(r, 1)], sem.at[s])

    def start_gather(blk, s):
        for r in range(EXPERT_BLOCK):
            row_copy(blk, s, r).start()

    @pl.when((i == 0) & (nused > 0))
    def _():
        start_gather(0, 0)

    @pl.when(i + 1 < nused)
    def _():
        start_gather(i + 1, 1 - slot)

    @pl.when(i < nused)
    def _():
        for r in range(EXPERT_BLOCK):
            row_copy(i, slot, r).wait()
        prev_e = blk_e_ref[jnp.maximum(i - 1, 0)]
        new_expert = (i == 0) | (blk_e_ref[i] != prev_e)

        @pl.when(new_expert)
        def _():
            wg_b[...] = wg_ref[...].astype(BF16)
            wu_b[...] = wu_ref[...].astype(BF16)
            wd_b[...] = wd_ref[...].astype(BF16)

        xb = xbuf[slot].astype(BF16)
        hg = _dot(xb, wg_b[...])
        hu = _dot(xb, wu_b[...])
        hid = (hg * jax.nn.sigmoid(hg) * hu).astype(BF16)
        y_ref[...] = _dot(hid, wd_b[...])

    @pl.when(i >= nused)
    def _():
        y_ref[...] = jnp.zeros_like(y_ref)


def _experts(x1, tok_pad, blk_e, nused, w_gate, w_up, w_down, layer):
    t, d = x1.shape
    n_blocks = blk_e.shape[0]
    ff = w_gate.shape[-1]
    wmap_in = lambda i, tok, be, nu: (layer, be[i], 0, 0)
    grid_spec = pltpu.PrefetchScalarGridSpec(
        num_scalar_prefetch=3,
        grid=(n_blocks,),
        in_specs=[pl.BlockSpec(memory_space=pl.ANY),
                  pl.BlockSpec((None, None, d, ff), wmap_in),
                  pl.BlockSpec((None, None, d, ff), wmap_in),
                  pl.BlockSpec((None, None, ff, d), wmap_in)],
        out_specs=pl.BlockSpec((EXPERT_BLOCK, d), lambda i, tok, be, nu: (i, 0)),
        scratch_shapes=[pltpu.VMEM((2, EXPERT_BLOCK, d), F32),
                        pltpu.SemaphoreType.DMA((2,)),
                        pltpu.VMEM((d, ff), BF16), pltpu.VMEM((d, ff), BF16),
                        pltpu.VMEM((ff, d), BF16)],
    )
    return pl.pallas_call(
        _experts_kernel,
        grid_spec=grid_spec,
        out_shape=jax.ShapeDtypeStruct((n_blocks * EXPERT_BLOCK, d), F32),
        compiler_params=_cparams(("arbitrary",)),
        name="experts",
    )(tok_pad, blk_e, nused, x1, w_gate, w_up, w_down)


def _final_kernel(pos_ref, ys_hbm, x_ref, wt_ref, p_ref, sg_ref, su_ref, sd_ref, pg_ref,
                  pp_ref, g_ref, b_ref, o_ref, ybuf, sem, *, alpha):
    tm = x_ref.shape[0]
    base = pl.program_id(0) * tm * TOP_K

    def row_copy(r, k):
        row = pos_ref[base + r * TOP_K + k]
        return pltpu.make_async_copy(ys_hbm.at[pl.ds(row, 1)], ybuf.at[k, pl.ds(r, 1)], sem)

    def start_rows(r, carry):
        for k in range(TOP_K):
            row_copy(r, k).start()
        return carry

    lax.fori_loop(0, tm, start_rows, 0)

    x = x_ref[...]
    xb = x.astype(BF16)
    hg = _dot(xb, sg_ref[...])
    hu = _dot(xb, su_ref[...])
    shared = _dot((hg * jax.nn.sigmoid(hg) * hu).astype(BF16), sd_ref[...])
    ple = jax.nn.sigmoid(_dot(xb, pg_ref[...])) * _dot(p_ref[...].astype(BF16), pp_ref[...])
    z = alpha * x + shared + ple

    def wait_rows(r, carry):
        for k in range(TOP_K):
            row_copy(r, k).wait()
        return carry

    lax.fori_loop(0, tm, wait_rows, 0)
    wt = wt_ref[...]
    for k in range(TOP_K):
        z = z + wt[:, k:k + 1] * ybuf[k]
    o_ref[...] = _layer_norm(z, g_ref[...], b_ref[...], LN_EPS)


def _final(pos, y_sorted, x1, wts, p, sg, su, sd, pg, pp, ln_g, ln_b, alpha, tm=128):
    t, d = x1.shape
    ff = sg.shape[1]
    pd = p.shape[1]
    row = lambda i, pos_ref: (i, 0)
    const = lambda i, pos_ref: (0, 0)
    one = pl.Buffered(1)
    grid_spec = pltpu.PrefetchScalarGridSpec(
        num_scalar_prefetch=1,
        grid=(t // tm,),
        in_specs=[pl.BlockSpec(memory_space=pl.ANY),
                  pl.BlockSpec((tm, d), row),
                  pl.BlockSpec((tm, TOP_K), row),
                  pl.BlockSpec((tm, pd), row),
                  pl.BlockSpec((d, ff), const, pipeline_mode=one),
                  pl.BlockSpec((d, ff), const, pipeline_mode=one),
                  pl.BlockSpec((ff, d), const, pipeline_mode=one),
                  pl.BlockSpec((d, d), const, pipeline_mode=one),
                  pl.BlockSpec((pd, d), const, pipeline_mode=one),
                  pl.BlockSpec((1, d), const), pl.BlockSpec((1, d), const)],
        out_specs=pl.BlockSpec((tm, d), row),
        scratch_shapes=[pltpu.VMEM((TOP_K, tm, d), F32), pltpu.SemaphoreType.DMA],
    )
    return pl.pallas_call(
        functools.partial(_final_kernel, alpha=alpha),
        grid_spec=grid_spec,
        out_shape=jax.ShapeDtypeStruct((t, d), F32),
        compiler_params=_cparams(("arbitrary",)),
        name="final",
    )(pos, y_sorted, x1, wts, p, sg, su, sd, pg, pp, ln_g, ln_b)


def _dispatch(idx, n_exp):
    t, k = idx.shape
    n_assign = t * k
    n_blocks = (n_assign + n_exp * (EXPERT_BLOCK - 1) + EXPERT_BLOCK - 1) // EXPERT_BLOCK
    n_rows = n_blocks * EXPERT_BLOCK
    flat_e = idx.reshape(-1)
    order = jnp.argsort(flat_e)
    sorted_e = flat_e[order]
    counts = jnp.zeros((n_exp,), jnp.int32).at[flat_e].add(1)
    start = jnp.cumsum(counts) - counts
    padded = (counts + EXPERT_BLOCK - 1) // EXPERT_BLOCK * EXPERT_BLOCK
    pad_end = jnp.cumsum(padded)
    pad_start = pad_end - padded
    dest = pad_start[sorted_e] + (jnp.arange(n_assign, dtype=jnp.int32) - start[sorted_e])
    tok_pad = jnp.zeros((n_rows,), jnp.int32).at[dest].set((order // k).astype(jnp.int32))
    pos = jnp.zeros((n_assign,), jnp.int32).at[order].set(dest.astype(jnp.int32))
    blk_e = jnp.minimum(
        jnp.searchsorted(pad_end, jnp.arange(n_blocks, dtype=jnp.int32) * EXPERT_BLOCK,
                         side="right"), n_exp - 1).astype(jnp.int32)
    nused = (pad_end[-1:] // EXPERT_BLOCK).astype(jnp.int32)
    return tok_pad, pos, blk_e, nused


def _pad_rows(w, rows):
    return jnp.pad(w, ((0, rows - w.shape[0]), (0, 0)))


def kernel(x, p, w_in, mu_shift, w0, w2, a0, a2, g2, k_k, k_a, r_k, lnx_g, lnx_b, sgu_ln_g, sgu_ln_b, sgu_w, sgu_b, w_out, ln1_g, ln1_b, router_w, router_bias, exp_gate, exp_up, exp_down, sh_gate, sh_up, sh_down, ple_gate_w, ple_proj, ln2_g, ln2_b):
    batch, seq, d = x.shape
    depth = w_in.shape[0]
    t = batch * seq
    width = d
    n_exp = router_w.shape[-1]
    alpha = float((2 * depth) ** 0.25)
    rwkv_cols = 3 * width + DECAY_LORA + ICLR_LORA + GATE_LORA
    xt = x.reshape(t, d)
    row = lambda vec: vec.reshape(1, -1)

    for i in range(depth):
        wi = w_in[i]
        o = 2 * width
        w_za = wi[:, :o].astype(BF16)
        w_rkv = wi[:, o:o + 3 * width]
        o2 = o + 3 * width
        w_zw = jnp.pad(wi[:, o2:o2 + DECAY_LORA], ((0, 0), (0, LANES - DECAY_LORA)))
        o3 = o2 + DECAY_LORA
        w_zaa = jnp.pad(wi[:, o3:o3 + ICLR_LORA], ((0, 0), (0, LANES - ICLR_LORA)))
        o4 = o3 + ICLR_LORA
        w_zg = wi[:, o4:o4 + GATE_LORA]
        w_proj = jnp.concatenate([w_rkv, w_zw, w_zaa, w_zg], axis=1).astype(BF16)
        w_gates = wi[:, o + rwkv_cols:].astype(BF16)
        ms = mu_shift[i]
        mu = jnp.concatenate([
            ms[:3 * width],
            jnp.pad(ms[3 * width:3 * width + DECAY_LORA], (0, LANES - DECAY_LORA)),
            jnp.pad(ms[3 * width + DECAY_LORA:3 * width + DECAY_LORA + ICLR_LORA],
                    (0, LANES - ICLR_LORA)),
            ms[3 * width + DECAY_LORA + ICLR_LORA:]]).reshape(1, -1)
        w2p = _pad_rows(w2[i], LANES).astype(BF16)
        a2p = _pad_rows(a2[i], LANES).astype(BF16)
        bias_full = jnp.repeat(sgu_b[i].T, width // SGU_GROUPS, axis=1)

        xb = xt.astype(BF16)
        za = _mm_act(xb, w_za, "gelu", BF16, 1024, 512)
        proj = _mm_act(xb, w_proj, None, F32, 1024, 512)
        gates = _mm_act(xb, w_gates, "sigmoid", BF16, 1024, 512)
        out_a = _sgu(za, row(sgu_ln_g[i]), row(sgu_ln_b[i]), sgu_w[i], bias_full)
        r, k, v, lw, a, g = _rwkv_prep(proj, mu, row(w0[i]), w2p, row(a0[i]), a2p,
                                       g2[i].astype(BF16), seq)
        out_b = _rwkv_chunks(r, k, v, lw, a, g, row(k_k[i]), row(k_a[i]), row(r_k[i]),
                             row(lnx_g[i]), row(lnx_b[i]), batch, seq)
        x1 = _mix_ln(out_a, out_b, w_out[i, :width].astype(BF16), w_out[i, width:].astype(BF16),
                     gates, xt, row(ln1_g[i]), row(ln1_b[i]), alpha)

        idx_t, wts_t = _router(x1, router_w[i].T.astype(BF16), router_bias[i].reshape(-1, 1))
        tok_pad, pos, blk_e, nused = _dispatch(idx_t.T, n_exp)
        y_sorted = _experts(x1, tok_pad, blk_e, nused, exp_gate, exp_up, exp_down, i)
        xt = _final(pos, y_sorted, x1, wts_t.T, p[i].reshape(t, -1),
                    sh_gate[i].astype(BF16), sh_up[i].astype(BF16), sh_down[i].astype(BF16),
                    ple_gate_w[i].astype(BF16), ple_proj[i].astype(BF16),
                    row(ln2_g[i]), row(ln2_b[i]), alpha)
    return xt.reshape(batch, seq, d)
```

```python
import functools
import math

import jax
import jax.numpy as jnp
from jax import lax
from jax.experimental import pallas as pl
from jax.experimental.pallas import tpu as pltpu

F32 = jnp.float32
BF16 = jnp.bfloat16

LANES = 128
VMEM_LIMIT = 56 * 1024 * 1024

SGU_CHUNK = 128
SGU_GROUPS = 16
HEAD = 64
RWKV_CHUNK = 64
HEAD_GROUP_LANES = 256
DECAY_LORA = 96
ICLR_LORA = 96
GATE_LORA = 256
LNX_EPS = 64e-5
LN_EPS = 1e-5
TOP_K = 8
N_GROUPS = 8
TOPK_GROUPS = 4
ROUTED_SCALE = 2.5
EXPERT_BLOCK = 128
INV_SQRT2 = 1.0 / math.sqrt(2.0)


def _cparams(sem, vmem=VMEM_LIMIT):
    return pltpu.CompilerParams(dimension_semantics=sem, vmem_limit_bytes=vmem)


def _dot(a, b):
    return jnp.dot(a, b, preferred_element_type=F32)


def _dot_nt(a, b):
    return lax.dot_general(a, b, (((1,), (1,)), ((), ())), preferred_element_type=F32)


def _layer_norm(z, g, b, eps):
    mu = jnp.mean(z, axis=-1, keepdims=True)
    zc = z - mu
    var = jnp.mean(zc * zc, axis=-1, keepdims=True)
    return zc * lax.rsqrt(var + eps) * g + b


def _mm_act_kernel(x_ref, w_ref, o_ref, *, act):
    acc = _dot(x_ref[...], w_ref[...])
    if act == "gelu":
        acc = 0.5 * acc * (1.0 + lax.erf(acc * INV_SQRT2))
    elif act == "sigmoid":
        acc = jax.nn.sigmoid(acc)
    o_ref[...] = acc.astype(o_ref.dtype)


def _mm_act(x, w, act, out_dtype, tm, tn):
    m, k = x.shape
    n = w.shape[1]
    tm = min(tm, m)
    return pl.pallas_call(
        functools.partial(_mm_act_kernel, act=act),
        grid=(m // tm, n // tn),
        in_specs=[pl.BlockSpec((tm, k), lambda i, j: (i, 0)),
                  pl.BlockSpec((k, tn), lambda i, j: (0, j))],
        out_specs=pl.BlockSpec((tm, tn), lambda i, j: (i, j)),
        out_shape=jax.ShapeDtypeStruct((m, n), out_dtype),
        compiler_params=_cparams(("parallel", "parallel")),
        name="in_proj_" + str(act),
    )(x, w)


def _sgu_kernel(u_ref, v_ref, g_ref, b_ref, w_ref, bias_ref, o_ref):
    v = v_ref[...].astype(F32)
    vn = _layer_norm(v, g_ref[...], b_ref[...], LN_EPS).astype(BF16)
    ri = lax.broadcasted_iota(jnp.int32, (SGU_CHUNK, SGU_CHUNK), 0)
    ci = lax.broadcasted_iota(jnp.int32, (SGU_CHUNK, SGU_CHUNK), 1)
    causal = ri >= ci
    for g in range(SGU_GROUPS):
        sl = slice(g * LANES, (g + 1) * LANES)
        wg = jnp.where(causal, w_ref[g], 0.0).astype(BF16)
        vm = _dot(wg, vn[:, sl]) + bias_ref[:, sl]
        o_ref[:, sl] = (u_ref[:, sl].astype(F32) * vm).astype(o_ref.dtype)


def _sgu(za, ln_g, ln_b, w_s, bias_full):
    t = za.shape[0]
    width = za.shape[1] // 2
    return pl.pallas_call(
        _sgu_kernel,
        grid=(t // SGU_CHUNK,),
        in_specs=[pl.BlockSpec((SGU_CHUNK, width), lambda c: (c, 0)),
                  pl.BlockSpec((SGU_CHUNK, width), lambda c: (c, 1)),
                  pl.BlockSpec((1, width), lambda c: (0, 0)),
                  pl.BlockSpec((1, width), lambda c: (0, 0)),
                  pl.BlockSpec((SGU_GROUPS, SGU_CHUNK, SGU_CHUNK), lambda c: (0, 0, 0)),
                  pl.BlockSpec((SGU_CHUNK, width), lambda c: (0, 0))],
        out_specs=pl.BlockSpec((SGU_CHUNK, width), lambda c: (c, 0)),
        out_shape=jax.ShapeDtypeStruct((t, width), BF16),
        compiler_params=_cparams(("parallel",)),
        name="sgu",
    )(za, za, ln_g, ln_b, w_s, bias_full)


def _rwkv_prep_kernel(p_ref, prev_ref, mu_ref, w0_ref, w2_ref, a0_ref, a2_ref, g2_ref,
                      r_ref, k_ref, v_ref, lw_ref, a_ref, g_ref, *, tiles_per_seq, width):
    tm = p_ref.shape[0]
    first = (pl.program_id(0) % tiles_per_seq) == 0
    row0 = lax.broadcasted_iota(jnp.int32, (tm, 1), 0) == 0

    def shifted(lo, hi):
        p = p_ref[:, lo:hi]
        last = jnp.where(first, 0.0, prev_ref[7:8, lo:hi])
        prev = jnp.where(row0, last, pltpu.roll(p, 1, axis=0))
        return p + (prev - p) * mu_ref[:, lo:hi]

    r_ref[...] = shifted(0, width)
    k_ref[...] = shifted(width, 2 * width)
    v_ref[...] = shifted(2 * width, 3 * width)
    o = 3 * width
    zw = shifted(o, o + LANES)
    za = shifted(o + LANES, o + 2 * LANES)
    zg = shifted(o + 2 * LANES, o + 2 * LANES + GATE_LORA)
    dw = w0_ref[...] + _dot(jnp.tanh(zw).astype(BF16), w2_ref[...])
    w_log = -(jnp.maximum(-dw, 0.0) + jnp.log(1.0 + jnp.exp(-jnp.abs(dw)))) - 0.5
    lw_ref[...] = -jnp.exp(w_log)
    a_ref[...] = jax.nn.sigmoid(a0_ref[...] + _dot(za.astype(BF16), a2_ref[...]))
    g_ref[...] = _dot(jax.nn.sigmoid(zg).astype(BF16), g2_ref[...])


def _rwkv_prep(proj, mu, w0, w2p, a0, a2p, g2, seq, tm=128):
    t, pc = proj.shape
    width = w0.shape[1]
    row = lambda i: (i, 0)
    const = lambda i: (0, 0)
    out = jax.ShapeDtypeStruct((t, width), F32)
    return pl.pallas_call(
        functools.partial(_rwkv_prep_kernel, tiles_per_seq=seq // tm, width=width),
        grid=(t // tm,),
        in_specs=[pl.BlockSpec((tm, pc), row),
                  pl.BlockSpec((8, pc), lambda i: (jnp.maximum(i * (tm // 8) - 1, 0), 0)),
                  pl.BlockSpec((1, pc), const),
                  pl.BlockSpec((1, width), const),
                  pl.BlockSpec((LANES, width), const),
                  pl.BlockSpec((1, width), const),
                  pl.BlockSpec((LANES, width), const),
                  pl.BlockSpec((GATE_LORA, width), const)],
        out_specs=[pl.BlockSpec((tm, width), row)] * 6,
        out_shape=[out] * 6,
        compiler_params=_cparams(("parallel",)),
        name="rwkv_prep",
    )(proj, proj, mu, w0, w2p, a0, a2p, g2)


def _rwkv_chunk_kernel(r_ref, k_ref, v_ref, lw_ref, a_ref, g_ref,
                       kk_ref, ka_ref, rk_ref, lg_ref, lb_ref,
                       o_ref, h_scr):
    c = RWKV_CHUNK
    w = r_ref.shape[1]
    gw = HEAD_GROUP_LANES
    ng = w // gw
    hpg = gw // HEAD

    @pl.when(pl.program_id(1) == 0)
    def _():
        h_scr[...] = jnp.zeros_like(h_scr)

    bi = lax.broadcasted_iota(jnp.int32, (gw, gw), 0) // HEAD
    bj = lax.broadcasted_iota(jnp.int32, (gw, gw), 1) // HEAD
    block_mask = bi == bj
    block_ones = block_mask.astype(BF16)

    def bd(y):
        return jnp.where(block_mask, jnp.concatenate([y] * hpg, axis=0), jnp.zeros((), y.dtype))

    ti = lax.broadcasted_iota(jnp.int32, (c, gw), 0)
    tj = lax.broadcasted_iota(jnp.int32, (c, gw), 1) % HEAD
    strict = ti > tj
    incl = ti >= tj
    eye = ti == tj
    eye_f = eye.astype(F32)
    eye_b = eye.astype(BF16)
    groups = range(ng)
    gsl = [slice(g * gw, (g + 1) * gw) for g in groups]

    def head_sum(x):
        hi = x.astype(BF16)
        lo = (x - hi.astype(F32)).astype(BF16)
        return jnp.concatenate(
            [_dot(hi[:, s], block_ones) + _dot(lo[:, s], block_ones) for s in gsl], axis=1)

    r = r_ref[...]
    k = k_ref[...]
    v = v_ref[...]
    lw = lw_ref[...]
    a = a_ref[...]

    t0 = lax.broadcasted_iota(jnp.int32, (c, c), 0)
    t1 = lax.broadcasted_iota(jnp.int32, (c, c), 1)
    gcum = jnp.dot((t0 >= t1).astype(F32), lw, precision=lax.Precision.HIGHEST,
                   preferred_element_type=F32)
    g_last = gcum[c - 1:c, :]
    e_g = jnp.exp(gcum)
    e_ng = jnp.exp(-gcum)
    e_gp = jnp.exp(gcum - lw)
    e_h = jnp.exp(g_last - gcum)
    e_last = jnp.exp(g_last)

    kk = k * kk_ref[...]
    kk = kk / jnp.maximum(jnp.sqrt(head_sum(kk * kk)), 1e-12)
    kmod = k * (1.0 + (a - 1.0) * ka_ref[...])
    kb = kk * a
    bonus = head_sum(r * kmod * rk_ref[...]) * v

    at_all = (-kk * e_gp).astype(BF16)
    bt_all = (kb * e_ng).astype(BF16)
    kt_all = (kmod * e_ng).astype(BF16)
    rt_all = r * e_g
    bh_all = (kb * e_h).astype(BF16)
    kh_all = (kmod * e_h).astype(BF16)
    v_all = v.astype(BF16)

    at = [at_all[:, s] for s in gsl]
    rt = [rt_all[:, s] for s in gsl]
    rtb = [x.astype(BF16) for x in rt]
    bt_d = [bd(bt_all[:, s]) for s in gsl]
    kt_d = [bd(kt_all[:, s]) for s in gsl]
    vb_d = [bd(v_all[:, s]) for s in gsl]
    a_ab = [jnp.where(strict, _dot_nt(at[g], bt_d[g]), 0.0) for g in groups]
    a_ak = [jnp.where(strict, _dot_nt(at[g], kt_d[g]), 0.0).astype(BF16) for g in groups]
    m_rb = [jnp.where(incl, _dot_nt(rtb[g], bt_d[g]), 0.0).astype(BF16) for g in groups]
    m_rk = [jnp.where(incl, _dot_nt(rtb[g], kt_d[g]), 0.0).astype(BF16) for g in groups]
    tinv = [eye_f + x for x in a_ab]
    pw = a_ab
    for _ in range(5):
        pwb = [x.astype(BF16) for x in pw]
        pw = [_dot(pwb[g], bd(pwb[g])) for g in groups]
        tinv = [tinv[g] + _dot(tinv[g].astype(BF16), bd(pw[g].astype(BF16))) for g in groups]
    tb = [x.astype(BF16) for x in tinv]
    w1 = [_dot(tb[g], bd(at[g])).astype(BF16) for g in groups]
    av = [_dot(a_ak[g], vb_d[g]).astype(BF16) for g in groups]
    x2 = [_dot(tb[g], bd(av[g])).astype(BF16) for g in groups]
    w1_d = [bd(x) for x in w1]
    x2_d = [bd(x) for x in x2]
    q1 = [(rt[g] + _dot(m_rb[g], w1_d[g])).astype(BF16) for g in groups]
    o2 = [_dot(m_rb[g], x2_d[g]) + _dot(m_rk[g], vb_d[g]) for g in groups]
    bh_t = [_dot_nt(eye_b, bd(bh_all[:, s])).astype(BF16) for s in gsl]
    kh_t = [_dot_nt(eye_b, bd(kh_all[:, s])).astype(BF16) for s in gsl]
    gmat = [(jnp.where(eye, e_last[:, gsl[g]], 0.0) + _dot(bh_t[g], w1_d[g])).astype(BF16)
            for g in groups]
    dmat = [_dot(bh_t[g], x2_d[g]) + _dot(kh_t[g], vb_d[g]) for g in groups]
    h_d = [bd(h_scr[g].astype(BF16)) for g in groups]
    y = jnp.concatenate([_dot(q1[g], h_d[g]) + o2[g] for g in groups], axis=1)
    for g in groups:
        h_scr[g] = _dot(gmat[g], h_d[g]) + dmat[g]

    ym = head_sum(y) * (1.0 / HEAD)
    yc = y - ym
    yv = head_sum(yc * yc) * (1.0 / HEAD)
    yn = yc * lax.rsqrt(yv + LNX_EPS) * lg_ref[...] + lb_ref[...]
    o_ref[...] = ((yn + bonus) * g_ref[...]).astype(o_ref.dtype)


def _rwkv_chunks(r, k, v, lw, a, g, k_k, k_a, r_k, lnx_g, lnx_b, batch, seq):
    t, width = r.shape
    nc = seq // RWKV_CHUNK
    tok = pl.BlockSpec((RWKV_CHUNK, width), lambda b, c: (b * nc + c, 0))
    par = pl.BlockSpec((1, width), lambda b, c: (0, 0))
    return pl.pallas_call(
        _rwkv_chunk_kernel,
        grid=(batch, nc),
        in_specs=[tok] * 6 + [par] * 5,
        out_specs=tok,
        out_shape=jax.ShapeDtypeStruct((t, width), BF16),
        scratch_shapes=[pltpu.VMEM((width // HEAD_GROUP_LANES, HEAD, HEAD_GROUP_LANES), F32)],
        compiler_params=_cparams(("parallel", "arbitrary")),
        name="rwkv_chunks",
    )(r, k, v, lw, a, g, k_k, k_a, r_k, lnx_g, lnx_b)


def _mix_ln_kernel(oa_ref, ob_ref, wa_ref, wb_ref, ga_ref, gb_ref, x_ref, g_ref, b_ref,
                   x1_ref, *, alpha):
    ma = _dot(oa_ref[...], wa_ref[...])
    mb = _dot(ob_ref[...], wb_ref[...])
    mix = ga_ref[...].astype(F32) * ma + gb_ref[...].astype(F32) * mb
    z = alpha * x_ref[...] + mix
    x1_ref[...] = _layer_norm(z, g_ref[...], b_ref[...], LN_EPS)


def _mix_ln(out_a, out_b, wa, wb, gates, x, ln_g, ln_b, alpha, tm=256):
    t, d = x.shape
    row = lambda i: (i, 0)
    const = lambda i: (0, 0)
    return pl.pallas_call(
        functools.partial(_mix_ln_kernel, alpha=alpha),
        grid=(t // tm,),
        in_specs=[pl.BlockSpec((tm, d), row), pl.BlockSpec((tm, d), row),
                  pl.BlockSpec((d, d), const, pipeline_mode=pl.Buffered(1)),
                  pl.BlockSpec((d, d), const, pipeline_mode=pl.Buffered(1)),
                  pl.BlockSpec((tm, d), row), pl.BlockSpec((tm, d), lambda i: (i, 1)),
                  pl.BlockSpec((tm, d), row),
                  pl.BlockSpec((1, d), const), pl.BlockSpec((1, d), const)],
        out_specs=pl.BlockSpec((tm, d), row),
        out_shape=jax.ShapeDtypeStruct((t, d), F32),
        compiler_params=_cparams(("parallel",)),
        name="mix_ln",
    )(out_a, out_b, wa, wb, gates, gates, x, ln_g, ln_b)


def _router_kernel(x_ref, wr_ref, bias_ref, idx_ref, wt_ref):
    n_exp = wr_ref.shape[0]
    tm = x_ref.shape[0]
    gsz = n_exp // N_GROUPS
    scores = jax.nn.sigmoid(_dot_nt(wr_ref[...], x_ref[...].astype(BF16)))
    biased = scores + bias_ref[...]
    neg = -jnp.inf

    grp_scores = []
    rows_g = lax.broadcasted_iota(jnp.int32, (gsz, tm), 0)
    for g in range(N_GROUPS):
        blk = biased[g * gsz:(g + 1) * gsz]
        m1 = jnp.max(blk, axis=0, keepdims=True)
        i1 = jnp.min(jnp.where(blk == m1, rows_g, gsz), axis=0, keepdims=True)
        m2 = jnp.max(jnp.where(rows_g == i1, neg, blk), axis=0, keepdims=True)
        grp_scores.append(m1 + m2)
    masked = []
    for g in range(N_GROUPS):
        rank = jnp.zeros((1, tm), jnp.int32)
        for g2 in range(N_GROUPS):
            if g2 == g:
                continue
            ahead = grp_scores[g2] > grp_scores[g]
            if g2 < g:
                ahead = ahead | (grp_scores[g2] == grp_scores[g])
            rank = rank + ahead.astype(jnp.int32)
        blk = biased[g * gsz:(g + 1) * gsz]
        masked.append(jnp.where(rank < TOPK_GROUPS, blk, neg))
    cand = jnp.concatenate(masked, axis=0)

    rows = lax.broadcasted_iota(jnp.int32, (n_exp, tm), 0)
    ids, wts = [], []
    for _ in range(TOP_K):
        m = jnp.max(cand, axis=0, keepdims=True)
        i = jnp.min(jnp.where(cand == m, rows, n_exp), axis=0, keepdims=True)
        hit = rows == i
        ids.append(i)
        wts.append(jnp.sum(jnp.where(hit, scores, 0.0), axis=0, keepdims=True))
        cand = jnp.where(hit, neg, cand)
    wt = jnp.concatenate(wts, axis=0)
    wt = wt / jnp.sum(wt, axis=0, keepdims=True) * ROUTED_SCALE
    idx_ref[...] = jnp.concatenate(ids, axis=0)
    wt_ref[...] = wt


def _router(x1, wr_t, bias_col, tm=512):
    t, d = x1.shape
    n_exp = wr_t.shape[0]
    return pl.pallas_call(
        _router_kernel,
        grid=(t // tm,),
        in_specs=[pl.BlockSpec((tm, d), lambda i: (i, 0)),
                  pl.BlockSpec((n_exp, d), lambda i: (0, 0)),
                  pl.BlockSpec((n_exp, 1), lambda i: (0, 0))],
        out_specs=[pl.BlockSpec((TOP_K, tm), lambda i: (0, i)),
                   pl.BlockSpec((TOP_K, tm), lambda i: (0, i))],
        out_shape=[jax.ShapeDtypeStruct((TOP_K, t), jnp.int32),
                   jax.ShapeDtypeStruct((TOP_K, t), F32)],
        compiler_params=_cparams(("parallel",)),
        name="router",
    )(x1, wr_t, bias_col)


BLK_FIELDS = 8
M_EXPERT, M_BASE, M_VALID, M_FIRST, M_WSLOT, M_NEXT = range(6)


def _experts_kernel(order_ref, meta_ref, nused_ref, x_hbm, wg_hbm, wu_hbm, wd_hbm, y_hbm,
                    xbuf, ybuf, wg_f, wu_f, wd_f, wg_b, wu_b, wd_b, gsem, ssem, wsem,
                    *, layer, n_exp):
    i = pl.program_id(0)
    nused = nused_ref[0]
    n_assign = order_ref.shape[0]
    slot = i % 2

    def meta(b, j):
        return meta_ref[b * BLK_FIELDS + j]

    def assignment(b, r):
        return order_ref[jnp.minimum(meta(b, M_BASE) + r, n_assign - 1)]

    def gather_copy(tok, s, r):
        return pltpu.make_async_copy(x_hbm.at[pl.ds(tok, 1)], xbuf.at[s, pl.ds(r, 1)], gsem.at[s])

    def scatter_copy(dst, s, r):
        return pltpu.make_async_copy(ybuf.at[s, pl.ds(r, 1)], y_hbm.at[pl.ds(dst, 1)], ssem.at[s])

    def weight_copies(e, ws):
        return (pltpu.make_async_copy(wg_hbm.at[layer, e], wg_f.at[ws], wsem.at[ws]),
                pltpu.make_async_copy(wu_hbm.at[layer, e], wu_f.at[ws], wsem.at[ws]),
                pltpu.make_async_copy(wd_hbm.at[layer, e], wd_f.at[ws], wsem.at[ws]))

    def start_gather(b, s):
        for r in range(EXPERT_BLOCK):
            tok = lax.shift_right_logical(assignment(b, r), TOP_K.bit_length() - 1)
            gather_copy(tok, s, r).start()

    def wait_scatter(s):
        for r in range(EXPERT_BLOCK):
            scatter_copy(0, s, r).wait()

    @pl.when(i == 0)
    def _():
        ybuf[0] = jnp.zeros(ybuf.shape[1:], ybuf.dtype)
        for s in range(2):
            dump = pltpu.make_async_copy(
                ybuf.at[0], y_hbm.at[pl.ds(n_assign + s * EXPERT_BLOCK, EXPERT_BLOCK)], ssem.at[0])
            dump.start()
            dump.wait()

    @pl.when((i == 0) & (nused > 0))
    def _():
        start_gather(0, 0)
        for cp in weight_copies(meta(0, M_EXPERT), 0):
            cp.start()

    @pl.when(i < nused)
    def _():
        ws = meta(i, M_WSLOT)

        @pl.when(meta(i, M_FIRST) == 1)
        def _():
            for cp in weight_copies(0, ws):
                cp.wait()
            nxt = meta(i, M_NEXT)

            @pl.when(nxt < n_exp)
            def _():
                for cp in weight_copies(nxt, 1 - ws):
                    cp.start()

            wg_b[...] = wg_f[ws].astype(BF16)
            wu_b[...] = wu_f[ws].astype(BF16)
            wd_b[...] = wd_f[ws].astype(BF16)

        @pl.when(i >= 2)
        def _():
            wait_scatter(slot)

        for r in range(EXPERT_BLOCK):
            gather_copy(0, slot, r).wait()

        @pl.when(i + 1 < nused)
        def _():
            start_gather(i + 1, 1 - slot)

        xb = xbuf[slot].astype(BF16)
        hg = _dot(xb, wg_b[...])
        hu = _dot(xb, wu_b[...])
        hid = (hg * jax.nn.sigmoid(hg) * hu).astype(BF16)
        ybuf[slot] = _dot(hid, wd_b[...])
        nvalid = meta(i, M_VALID)
        for r in range(EXPERT_BLOCK):
            dst = jnp.where(r < nvalid, assignment(i, r), n_assign + slot * EXPERT_BLOCK + r)
            scatter_copy(dst, slot, r).start()

        @pl.when(i == nused - 1)
        def _():
            wait_scatter(slot)

            @pl.when(i >= 1)
            def _():
                wait_scatter(1 - slot)


def _experts(x1, order, meta, nused, w_gate, w_up, w_down, layer):
    t, d = x1.shape
    n_blocks = meta.shape[0] // BLK_FIELDS
    n_exp, ff = w_gate.shape[1], w_gate.shape[-1]
    n_assign = order.shape[0]
    any_spec = pl.BlockSpec(memory_space=pl.ANY)
    grid_spec = pltpu.PrefetchScalarGridSpec(
        num_scalar_prefetch=3,
        grid=(n_blocks,),
        in_specs=[any_spec] * 4,
        out_specs=any_spec,
        scratch_shapes=[pltpu.VMEM((2, EXPERT_BLOCK, d), F32),
                        pltpu.VMEM((2, EXPERT_BLOCK, d), F32),
                        pltpu.VMEM((2, d, ff), F32), pltpu.VMEM((2, d, ff), F32),
                        pltpu.VMEM((2, ff, d), F32),
                        pltpu.VMEM((d, ff), BF16), pltpu.VMEM((d, ff), BF16),
                        pltpu.VMEM((ff, d), BF16),
                        pltpu.SemaphoreType.DMA((2,)), pltpu.SemaphoreType.DMA((2,)),
                        pltpu.SemaphoreType.DMA((2,))],
    )
    return pl.pallas_call(
        functools.partial(_experts_kernel, layer=layer, n_exp=n_exp),
        grid_spec=grid_spec,
        out_shape=jax.ShapeDtypeStruct((n_assign + 2 * EXPERT_BLOCK, d), F32),
        compiler_params=_cparams(("arbitrary",)),
        name="experts",
    )(order, meta, nused, x1, w_gate, w_up, w_down)


def _final_kernel(y_ref, wcol_ref, x_ref, p_ref, sg_ref, su_ref, sd_ref, pg_ref, pp_ref,
                  g_ref, b_ref, o_ref, *, alpha):
    tm = x_ref.shape[0]
    x = x_ref[...]
    xb = x.astype(BF16)
    hg = _dot(xb, sg_ref[...])
    hu = _dot(xb, su_ref[...])
    shared = _dot((hg * jax.nn.sigmoid(hg) * hu).astype(BF16), sd_ref[...])
    ple = jax.nn.sigmoid(_dot(xb, pg_ref[...])) * _dot(p_ref[...].astype(BF16), pp_ref[...])
    yw = y_ref[...] * wcol_ref[...]
    hi = yw.astype(BF16)
    lo = (yw - hi.astype(F32)).astype(BF16)
    ri = lax.broadcasted_iota(jnp.int32, (tm, tm * TOP_K), 0)
    ci = lax.broadcasted_iota(jnp.int32, (tm, tm * TOP_K), 1)
    sel = (lax.shift_right_logical(ci, TOP_K.bit_length() - 1) == ri).astype(BF16)
    routed = _dot(sel, hi) + _dot(sel, lo)
    z = alpha * x + routed + shared + ple
    o_ref[...] = _layer_norm(z, g_ref[...], b_ref[...], LN_EPS)


def _final(y_assign, wcol, x1, p, sg, su, sd, pg, pp, ln_g, ln_b, alpha, tm=128):
    t, d = x1.shape
    ff = sg.shape[1]
    pd = p.shape[1]
    row = lambda i: (i, 0)
    const = lambda i: (0, 0)
    one = pl.Buffered(1)
    return pl.pallas_call(
        functools.partial(_final_kernel, alpha=alpha),
        grid=(t // tm,),
        in_specs=[pl.BlockSpec((tm * TOP_K, d), row),
                  pl.BlockSpec((tm * TOP_K, 1), row),
                  pl.BlockSpec((tm, d), row),
                  pl.BlockSpec((tm, pd), row),
                  pl.BlockSpec((d, ff), const, pipeline_mode=one),
                  pl.BlockSpec((d, ff), const, pipeline_mode=one),
                  pl.BlockSpec((ff, d), const, pipeline_mode=one),
                  pl.BlockSpec((d, d), const, pipeline_mode=one),
                  pl.BlockSpec((pd, d), const, pipeline_mode=one),
                  pl.BlockSpec((1, d), const), pl.BlockSpec((1, d), const)],
        out_specs=pl.BlockSpec((tm, d), row),
        out_shape=jax.ShapeDtypeStruct((t, d), F32),
        compiler_params=_cparams(("parallel",)),
        name="final",
    )(y_assign, wcol, x1, p, sg, su, sd, pg, pp, ln_g, ln_b)


def _dispatch(idx_t, n_exp):
    k, t = idx_t.shape
    n_assign = t * k
    n_blocks = (n_assign + n_exp * (EXPERT_BLOCK - 1) + EXPERT_BLOCK - 1) // EXPERT_BLOCK
    i32 = jnp.int32
    flat_e = idx_t.T.reshape(-1)
    sorted_e, order = lax.sort_key_val(flat_e, jnp.arange(n_assign, dtype=i32))
    start = jnp.searchsorted(sorted_e, jnp.arange(n_exp, dtype=i32), side="left").astype(i32)
    end = jnp.concatenate([start[1:], jnp.full((1,), n_assign, i32)])
    counts = end - start
    nblk = (counts + EXPERT_BLOCK - 1) // EXPERT_BLOCK
    blk_end = jnp.cumsum(nblk).astype(i32)
    blk_start = blk_end - nblk
    nused = blk_end[-1:]
    b = jnp.arange(n_blocks, dtype=i32)
    e_of_b = jnp.minimum(jnp.searchsorted(blk_end, b, side="right"), n_exp - 1).astype(i32)
    base = start[e_of_b] + (b - blk_start[e_of_b]) * EXPERT_BLOCK
    nvalid = jnp.clip(end[e_of_b] - base, 0, EXPERT_BLOCK)
    first = (b == blk_start[e_of_b]).astype(i32)
    used = counts > 0
    wslot = ((jnp.cumsum(used.astype(i32)) - 1) % 2).astype(i32)
    cand = jnp.where(used, jnp.arange(n_exp, dtype=i32), n_exp)
    nxt = jnp.concatenate([lax.cummin(cand, axis=0, reverse=True)[1:], jnp.full((1,), n_exp, i32)])
    zeros = jnp.zeros_like(b)
    table = jnp.stack([e_of_b, base, nvalid, first, wslot[e_of_b], nxt[e_of_b], zeros, zeros],
                      axis=1).reshape(-1).astype(i32)
    return order, table, nused.astype(i32)


def _pad_rows(w, rows):
    return jnp.pad(w, ((0, rows - w.shape[0]), (0, 0)))


def kernel(x, p, w_in, mu_shift, w0, w2, a0, a2, g2, k_k, k_a, r_k, lnx_g, lnx_b, sgu_ln_g, sgu_ln_b, sgu_w, sgu_b, w_out, ln1_g, ln1_b, router_w, router_bias, exp_gate, exp_up, exp_down, sh_gate, sh_up, sh_down, ple_gate_w, ple_proj, ln2_g, ln2_b):
    batch, seq, d = x.shape
    depth = w_in.shape[0]
    t = batch * seq
    width = d
    n_exp = router_w.shape[-1]
    alpha = float((2 * depth) ** 0.25)
    rwkv_cols = 3 * width + DECAY_LORA + ICLR_LORA + GATE_LORA
    xt = x.reshape(t, d)
    row = lambda vec: vec.reshape(1, -1)

    for i in range(depth):
        wi = w_in[i]
        o = 2 * width
        w_za = wi[:, :o].astype(BF16)
        w_rkv = wi[:, o:o + 3 * width]
        o2 = o + 3 * width
        w_zw = jnp.pad(wi[:, o2:o2 + DECAY_LORA], ((0, 0), (0, LANES - DECAY_LORA)))
        o3 = o2 + DECAY_LORA
        w_zaa = jnp.pad(wi[:, o3:o3 + ICLR_LORA], ((0, 0), (0, LANES - ICLR_LORA)))
        o4 = o3 + ICLR_LORA
        w_zg = wi[:, o4:o4 + GATE_LORA]
        w_proj = jnp.concatenate([w_rkv, w_zw, w_zaa, w_zg], axis=1).astype(BF16)
        w_gates = wi[:, o + rwkv_cols:].astype(BF16)
        ms = mu_shift[i]
        mu = jnp.concatenate([
            ms[:3 * width],
            jnp.pad(ms[3 * width:3 * width + DECAY_LORA], (0, LANES - DECAY_LORA)),
            jnp.pad(ms[3 * width + DECAY_LORA:3 * width + DECAY_LORA + ICLR_LORA],
                    (0, LANES - ICLR_LORA)),
            ms[3 * width + DECAY_LORA + ICLR_LORA:]]).reshape(1, -1)
        w2p = _pad_rows(w2[i], LANES).astype(BF16)
        a2p = _pad_rows(a2[i], LANES).astype(BF16)
        bias_full = jnp.repeat(sgu_b[i].T, width // SGU_GROUPS, axis=1)

        xb = xt.astype(BF16)
        za = _mm_act(xb, w_za, "gelu", BF16, 1024, 512)
        proj = _mm_act(xb, w_proj, None, F32, 1024, 512)
        gates = _mm_act(xb, w_gates, "sigmoid", BF16, 1024, 512)
        out_a = _sgu(za, row(sgu_ln_g[i]), row(sgu_ln_b[i]), sgu_w[i], bias_full)
        r, k, v, lw, a, g = _rwkv_prep(proj, mu, row(w0[i]), w2p, row(a0[i]), a2p,
                                       g2[i].astype(BF16), seq)
        out_b = _rwkv_chunks(r, k, v, lw, a, g, row(k_k[i]), row(k_a[i]), row(r_k[i]),
                             row(lnx_g[i]), row(lnx_b[i]), batch, seq)
        x1 = _mix_ln(out_a, out_b, w_out[i, :width].astype(BF16), w_out[i, width:].astype(BF16),
                     gates, xt, row(ln1_g[i]), row(ln1_b[i]), alpha)

        idx_t, wts_t = _router(x1, router_w[i].T.astype(BF16), router_bias[i].reshape(-1, 1))
        order, table, nused = _dispatch(idx_t, n_exp)
        y_assign = _experts(x1, order, table, nused, exp_gate, exp_up, exp_down, i)
        xt = _final(y_assign, wts_t.T.reshape(-1, 1), x1, p[i].reshape(t, -1),
                    sh_gate[i].astype(BF16), sh_up[i].astype(BF16), sh_down[i].astype(BF16),
                    ple_gate_w[i].astype(BF16), ple_proj[i].astype(BF16),
                    row(ln2_g[i]), row(ln2_b[i]), alpha)
    return xt.reshape(batch, seq, d)
```

```python
import functools
import math

import jax
import jax.numpy as jnp
from jax import lax
from jax.experimental import pallas as pl
from jax.experimental.pallas import tpu as pltpu

F32 = jnp.float32
BF16 = jnp.bfloat16

LANES = 128
VMEM_LIMIT = 56 * 1024 * 1024

SGU_CHUNK = 128
SGU_GROUPS = 16
HEAD = 64
RWKV_CHUNK = 64
HEAD_GROUP_LANES = 256
DECAY_LORA = 96
ICLR_LORA = 96
GATE_LORA = 256
LNX_EPS = 64e-5
LN_EPS = 1e-5
TOP_K = 8
N_GROUPS = 8
TOPK_GROUPS = 4
ROUTED_SCALE = 2.5
EXPERT_BLOCK = 128
INV_SQRT2 = 1.0 / math.sqrt(2.0)


def _cparams(sem, vmem=VMEM_LIMIT):
    return pltpu.CompilerParams(dimension_semantics=sem, vmem_limit_bytes=vmem)


def _dot(a, b):
    return jnp.dot(a, b, preferred_element_type=F32)


def _dot_nt(a, b):
    return lax.dot_general(a, b, (((1,), (1,)), ((), ())), preferred_element_type=F32)


def _layer_norm(z, g, b, eps):
    mu = jnp.mean(z, axis=-1, keepdims=True)
    zc = z - mu
    var = jnp.mean(zc * zc, axis=-1, keepdims=True)
    return zc * lax.rsqrt(var + eps) * g + b


def _mm_act_kernel(x_ref, w_ref, o_ref, *, act):
    acc = _dot(x_ref[...], w_ref[...])
    if act == "gelu":
        acc = 0.5 * acc * (1.0 + lax.erf(acc * INV_SQRT2))
    elif act == "sigmoid":
        acc = jax.nn.sigmoid(acc)
    o_ref[...] = acc.astype(o_ref.dtype)


def _mm_act(x, w, act, out_dtype, tm, tn):
    m, k = x.shape
    n = w.shape[1]
    tm = min(tm, m)
    return pl.pallas_call(
        functools.partial(_mm_act_kernel, act=act),
        grid=(m // tm, n // tn),
        in_specs=[pl.BlockSpec((tm, k), lambda i, j: (i, 0)),
                  pl.BlockSpec((k, tn), lambda i, j: (0, j))],
        out_specs=pl.BlockSpec((tm, tn), lambda i, j: (i, j)),
        out_shape=jax.ShapeDtypeStruct((m, n), out_dtype),
        compiler_params=_cparams(("parallel", "parallel")),
        name="in_proj_" + str(act),
    )(x, w)


def _sgu_kernel(u_ref, v_ref, g_ref, b_ref, w_ref, bias_ref, o_ref):
    v = v_ref[...].astype(F32)
    vn = _layer_norm(v, g_ref[...], b_ref[...], LN_EPS).astype(BF16)
    ri = lax.broadcasted_iota(jnp.int32, (SGU_CHUNK, SGU_CHUNK), 0)
    ci = lax.broadcasted_iota(jnp.int32, (SGU_CHUNK, SGU_CHUNK), 1)
    causal = ri >= ci
    for g in range(SGU_GROUPS):
        sl = slice(g * LANES, (g + 1) * LANES)
        wg = jnp.where(causal, w_ref[g], 0.0).astype(BF16)
        vm = _dot(wg, vn[:, sl]) + bias_ref[:, sl]
        o_ref[:, sl] = (u_ref[:, sl].astype(F32) * vm).astype(o_ref.dtype)


def _sgu(za, ln_g, ln_b, w_s, bias_full):
    t = za.shape[0]
    width = za.shape[1] // 2
    return pl.pallas_call(
        _sgu_kernel,
        grid=(t // SGU_CHUNK,),
        in_specs=[pl.BlockSpec((SGU_CHUNK, width), lambda c: (c, 0)),
                  pl.BlockSpec((SGU_CHUNK, width), lambda c: (c, 1)),
                  pl.BlockSpec((1, width), lambda c: (0, 0)),
                  pl.BlockSpec((1, width), lambda c: (0, 0)),
                  pl.BlockSpec((SGU_GROUPS, SGU_CHUNK, SGU_CHUNK), lambda c: (0, 0, 0)),
                  pl.BlockSpec((SGU_CHUNK, width), lambda c: (0, 0))],
        out_specs=pl.BlockSpec((SGU_CHUNK, width), lambda c: (c, 0)),
        out_shape=jax.ShapeDtypeStruct((t, width), BF16),
        compiler_params=_cparams(("parallel",)),
        name="sgu",
    )(za, za, ln_g, ln_b, w_s, bias_full)


def _rwkv_prep_kernel(p_ref, prev_ref, mu_ref, w0_ref, w2_ref, a0_ref, a2_ref, g2_ref,
                      r_ref, k_ref, v_ref, lw_ref, a_ref, g_ref, *, tiles_per_seq, width):
    tm = p_ref.shape[0]
    first = (pl.program_id(0) % tiles_per_seq) == 0
    row0 = lax.broadcasted_iota(jnp.int32, (tm, 1), 0) == 0

    def shifted(lo, hi):
        p = p_ref[:, lo:hi]
        last = jnp.where(first, 0.0, prev_ref[7:8, lo:hi])
        prev = jnp.where(row0, last, pltpu.roll(p, 1, axis=0))
        return p + (prev - p) * mu_ref[:, lo:hi]

    r_ref[...] = shifted(0, width)
    k_ref[...] = shifted(width, 2 * width)
    v_ref[...] = shifted(2 * width, 3 * width)
    o = 3 * width
    zw = shifted(o, o + LANES)
    za = shifted(o + LANES, o + 2 * LANES)
    zg = shifted(o + 2 * LANES, o + 2 * LANES + GATE_LORA)
    dw = w0_ref[...] + _dot(jnp.tanh(zw).astype(BF16), w2_ref[...])
    w_log = -(jnp.maximum(-dw, 0.0) + jnp.log(1.0 + jnp.exp(-jnp.abs(dw)))) - 0.5
    lw_ref[...] = -jnp.exp(w_log)
    a_ref[...] = jax.nn.sigmoid(a0_ref[...] + _dot(za.astype(BF16), a2_ref[...]))
    g_ref[...] = _dot(jax.nn.sigmoid(zg).astype(BF16), g2_ref[...])


def _rwkv_prep(proj, mu, w0, w2p, a0, a2p, g2, seq, tm=128):
    t, pc = proj.shape
    width = w0.shape[1]
    row = lambda i: (i, 0)
    const = lambda i: (0, 0)
    out = jax.ShapeDtypeStruct((t, width), F32)
    return pl.pallas_call(
        functools.partial(_rwkv_prep_kernel, tiles_per_seq=seq // tm, width=width),
        grid=(t // tm,),
        in_specs=[pl.BlockSpec((tm, pc), row),
                  pl.BlockSpec((8, pc), lambda i: (jnp.maximum(i * (tm // 8) - 1, 0), 0)),
                  pl.BlockSpec((1, pc), const),
                  pl.BlockSpec((1, width), const),
                  pl.BlockSpec((LANES, width), const),
                  pl.BlockSpec((1, width), const),
                  pl.BlockSpec((LANES, width), const),
                  pl.BlockSpec((GATE_LORA, width), const)],
        out_specs=[pl.BlockSpec((tm, width), row)] * 6,
        out_shape=[out] * 6,
        compiler_params=_cparams(("parallel",)),
        name="rwkv_prep",
    )(proj, proj, mu, w0, w2p, a0, a2p, g2)


def _rwkv_chunk_kernel(r_ref, k_ref, v_ref, lw_ref, a_ref, g_ref,
                       kk_ref, ka_ref, rk_ref, lg_ref, lb_ref,
                       o_ref, h_scr):
    c = RWKV_CHUNK
    w = r_ref.shape[1]
    gw = HEAD_GROUP_LANES
    ng = w // gw
    hpg = gw // HEAD

    @pl.when(pl.program_id(1) == 0)
    def _():
        h_scr[...] = jnp.zeros_like(h_scr)

    bi = lax.broadcasted_iota(jnp.int32, (gw, gw), 0) // HEAD
    bj = lax.broadcasted_iota(jnp.int32, (gw, gw), 1) // HEAD
    block_mask = bi == bj
    block_ones = block_mask.astype(BF16)

    def bd(y):
        return jnp.where(block_mask, jnp.concatenate([y] * hpg, axis=0), jnp.zeros((), y.dtype))

    ti = lax.broadcasted_iota(jnp.int32, (c, gw), 0)
    tj = lax.broadcasted_iota(jnp.int32, (c, gw), 1) % HEAD
    strict = ti > tj
    incl = ti >= tj
    eye = ti == tj
    eye_f = eye.astype(F32)
    eye_b = eye.astype(BF16)
    groups = range(ng)
    gsl = [slice(g * gw, (g + 1) * gw) for g in groups]

    def head_sum(x):
        hi = x.astype(BF16)
        lo = (x - hi.astype(F32)).astype(BF16)
        return jnp.concatenate(
            [_dot(hi[:, s], block_ones) + _dot(lo[:, s], block_ones) for s in gsl], axis=1)

    r = r_ref[...]
    k = k_ref[...]
    v = v_ref[...]
    lw = lw_ref[...]
    a = a_ref[...]

    t0 = lax.broadcasted_iota(jnp.int32, (c, c), 0)
    t1 = lax.broadcasted_iota(jnp.int32, (c, c), 1)
    gcum = jnp.dot((t0 >= t1).astype(F32), lw, precision=lax.Precision.HIGHEST,
                   preferred_element_type=F32)
    g_last = gcum[c - 1:c, :]
    e_g = jnp.exp(gcum)
    e_ng = jnp.exp(-gcum)
    e_gp = jnp.exp(gcum - lw)
    e_h = jnp.exp(g_last - gcum)
    e_last = jnp.exp(g_last)

    kk = k * kk_ref[...]
    kk = kk / jnp.maximum(jnp.sqrt(head_sum(kk * kk)), 1e-12)
    kmod = k * (1.0 + (a - 1.0) * ka_ref[...])
    kb = kk * a
    bonus = head_sum(r * kmod * rk_ref[...]) * v

    at_all = (-kk * e_gp).astype(BF16)
    bt_all = (kb * e_ng).astype(BF16)
    kt_all = (kmod * e_ng).astype(BF16)
    rt_all = r * e_g
    bh_all = (kb * e_h).astype(BF16)
    kh_all = (kmod * e_h).astype(BF16)
    v_all = v.astype(BF16)

    at = [at_all[:, s] for s in gsl]
    rt = [rt_all[:, s] for s in gsl]
    rtb = [x.astype(BF16) for x in rt]
    bt_d = [bd(bt_all[:, s]) for s in gsl]
    kt_d = [bd(kt_all[:, s]) for s in gsl]
    vb_d = [bd(v_all[:, s]) for s in gsl]
    a_ab = [jnp.where(strict, _dot_nt(at[g], bt_d[g]), 0.0) for g in groups]
    a_ak = [jnp.where(strict, _dot_nt(at[g], kt_d[g]), 0.0).astype(BF16) for g in groups]
    m_rb = [jnp.where(incl, _dot_nt(rtb[g], bt_d[g]), 0.0).astype(BF16) for g in groups]
    m_rk = [jnp.where(incl, _dot_nt(rtb[g], kt_d[g]), 0.0).astype(BF16) for g in groups]
    tinv = [eye_f + x for x in a_ab]
    pw = a_ab
    for _ in range(5):
        pwb = [x.astype(BF16) for x in pw]
        pw = [_dot(pwb[g], bd(pwb[g])) for g in groups]
        tinv = [tinv[g] + _dot(tinv[g].astype(BF16), bd(pw[g].astype(BF16))) for g in groups]
    tb = [x.astype(BF16) for x in tinv]
    w1 = [_dot(tb[g], bd(at[g])).astype(BF16) for g in groups]
    av = [_dot(a_ak[g], vb_d[g]).astype(BF16) for g in groups]
    x2 = [_dot(tb[g], bd(av[g])).astype(BF16) for g in groups]
    w1_d = [bd(x) for x in w1]
    x2_d = [bd(x) for x in x2]
    q1 = [(rt[g] + _dot(m_rb[g], w1_d[g])).astype(BF16) for g in groups]
    o2 = [_dot(m_rb[g], x2_d[g]) + _dot(m_rk[g], vb_d[g]) for g in groups]
    bh_t = [_dot_nt(eye_b, bd(bh_all[:, s])).astype(BF16) for s in gsl]
    kh_t = [_dot_nt(eye_b, bd(kh_all[:, s])).astype(BF16) for s in gsl]
    gmat = [(jnp.where(eye, e_last[:, gsl[g]], 0.0) + _dot(bh_t[g], w1_d[g])).astype(BF16)
            for g in groups]
    dmat = [_dot(bh_t[g], x2_d[g]) + _dot(kh_t[g], vb_d[g]) for g in groups]
    h_d = [bd(h_scr[g].astype(BF16)) for g in groups]
    y = jnp.concatenate([_dot(q1[g], h_d[g]) + o2[g] for g in groups], axis=1)
    for g in groups:
        h_scr[g] = _dot(gmat[g], h_d[g]) + dmat[g]

    ym = head_sum(y) * (1.0 / HEAD)
    yc = y - ym
    yv = head_sum(yc * yc) * (1.0 / HEAD)
    yn = yc * lax.rsqrt(yv + LNX_EPS) * lg_ref[...] + lb_ref[...]
    o_ref[...] = ((yn + bonus) * g_ref[...]).astype(o_ref.dtype)


def _rwkv_chunks(r, k, v, lw, a, g, k_k, k_a, r_k, lnx_g, lnx_b, batch, seq):
    t, width = r.shape
    nc = seq // RWKV_CHUNK
    tok = pl.BlockSpec((RWKV_CHUNK, width), lambda b, c: (b * nc + c, 0))
    par = pl.BlockSpec((1, width), lambda b, c: (0, 0))
    return pl.pallas_call(
        _rwkv_chunk_kernel,
        grid=(batch, nc),
        in_specs=[tok] * 6 + [par] * 5,
        out_specs=tok,
        out_shape=jax.ShapeDtypeStruct((t, width), BF16),
        scratch_shapes=[pltpu.VMEM((width // HEAD_GROUP_LANES, HEAD, HEAD_GROUP_LANES), F32)],
        compiler_params=_cparams(("parallel", "arbitrary")),
        name="rwkv_chunks",
    )(r, k, v, lw, a, g, k_k, k_a, r_k, lnx_g, lnx_b)


def _mix_ln_kernel(oa_ref, ob_ref, wa_ref, wb_ref, ga_ref, gb_ref, x_ref, g_ref, b_ref,
                   x1_ref, *, alpha):
    ma = _dot(oa_ref[...], wa_ref[...])
    mb = _dot(ob_ref[...], wb_ref[...])
    mix = ga_ref[...].astype(F32) * ma + gb_ref[...].astype(F32) * mb
    z = alpha * x_ref[...] + mix
    x1_ref[...] = _layer_norm(z, g_ref[...], b_ref[...], LN_EPS)


def _mix_ln(out_a, out_b, wa, wb, gates, x, ln_g, ln_b, alpha, tm=256):
    t, d = x.shape
    row = lambda i: (i, 0)
    const = lambda i: (0, 0)
    return pl.pallas_call(
        functools.partial(_mix_ln_kernel, alpha=alpha),
        grid=(t // tm,),
        in_specs=[pl.BlockSpec((tm, d), row), pl.BlockSpec((tm, d), row),
                  pl.BlockSpec((d, d), const, pipeline_mode=pl.Buffered(1)),
                  pl.BlockSpec((d, d), const, pipeline_mode=pl.Buffered(1)),
                  pl.BlockSpec((tm, d), row), pl.BlockSpec((tm, d), lambda i: (i, 1)),
                  pl.BlockSpec((tm, d), row),
                  pl.BlockSpec((1, d), const), pl.BlockSpec((1, d), const)],
        out_specs=pl.BlockSpec((tm, d), row),
        out_shape=jax.ShapeDtypeStruct((t, d), F32),
        compiler_params=_cparams(("parallel",)),
        name="mix_ln",
    )(out_a, out_b, wa, wb, gates, gates, x, ln_g, ln_b)


def _router_kernel(x_ref, wr_ref, bias_ref, idx_ref, wt_ref):
    n_exp = wr_ref.shape[0]
    tm = x_ref.shape[0]
    gsz = n_exp // N_GROUPS
    scores = jax.nn.sigmoid(_dot_nt(wr_ref[...], x_ref[...].astype(BF16)))
    biased = scores + bias_ref[...]
    neg = -jnp.inf

    grp_scores = []
    rows_g = lax.broadcasted_iota(jnp.int32, (gsz, tm), 0)
    for g in range(N_GROUPS):
        blk = biased[g * gsz:(g + 1) * gsz]
        m1 = jnp.max(blk, axis=0, keepdims=True)
        i1 = jnp.min(jnp.where(blk == m1, rows_g, gsz), axis=0, keepdims=True)
        m2 = jnp.max(jnp.where(rows_g == i1, neg, blk), axis=0, keepdims=True)
        grp_scores.append(m1 + m2)
    masked = []
    for g in range(N_GROUPS):
        rank = jnp.zeros((1, tm), jnp.int32)
        for g2 in range(N_GROUPS):
            if g2 == g:
                continue
            ahead = grp_scores[g2] > grp_scores[g]
            if g2 < g:
                ahead = ahead | (grp_scores[g2] == grp_scores[g])
            rank = rank + ahead.astype(jnp.int32)
        blk = biased[g * gsz:(g + 1) * gsz]
        masked.append(jnp.where(rank < TOPK_GROUPS, blk, neg))
    cand = jnp.concatenate(masked, axis=0)

    rows = lax.broadcasted_iota(jnp.int32, (n_exp, tm), 0)
    ids, wts = [], []
    for _ in range(TOP_K):
        m = jnp.max(cand, axis=0, keepdims=True)
        i = jnp.min(jnp.where(cand == m, rows, n_exp), axis=0, keepdims=True)
        hit = rows == i
        ids.append(i)
        wts.append(jnp.sum(jnp.where(hit, scores, 0.0), axis=0, keepdims=True))
        cand = jnp.where(hit, neg, cand)
    wt = jnp.concatenate(wts, axis=0)
    wt = wt / jnp.sum(wt, axis=0, keepdims=True) * ROUTED_SCALE
    idx_ref[...] = jnp.concatenate(ids, axis=0)
    wt_ref[...] = wt


def _router(x1, wr_t, bias_col, tm=512):
    t, d = x1.shape
    n_exp = wr_t.shape[0]
    return pl.pallas_call(
        _router_kernel,
        grid=(t // tm,),
        in_specs=[pl.BlockSpec((tm, d), lambda i: (i, 0)),
                  pl.BlockSpec((n_exp, d), lambda i: (0, 0)),
                  pl.BlockSpec((n_exp, 1), lambda i: (0, 0))],
        out_specs=[pl.BlockSpec((TOP_K, tm), lambda i: (0, i)),
                   pl.BlockSpec((TOP_K, tm), lambda i: (0, i))],
        out_shape=[jax.ShapeDtypeStruct((TOP_K, t), jnp.int32),
                   jax.ShapeDtypeStruct((TOP_K, t), F32)],
        compiler_params=_cparams(("parallel",)),
        name="router",
    )(x1, wr_t, bias_col)


BLK_FIELDS = 8
M_EXPERT, M_BASE, M_VALID, M_FIRST, M_WSLOT, M_NEXT = range(6)


def _experts_kernel(order_ref, meta_ref, nused_ref, x_hbm, wg_hbm, wu_hbm, wd_hbm, y_hbm,
                    xbuf, ybuf, wg_f, wu_f, wd_f, wg_b, wu_b, wd_b, gsem, ssem, wsem,
                    *, layer, n_exp):
    i = pl.program_id(0)
    nused = nused_ref[0]
    n_assign = order_ref.shape[0]
    slot = i % 2

    def meta(b, j):
        return meta_ref[b * BLK_FIELDS + j]

    def assignment(b, r):
        return order_ref[jnp.minimum(meta(b, M_BASE) + r, n_assign - 1)]

    def gather_copy(tok, s, r):
        return pltpu.make_async_copy(x_hbm.at[pl.ds(tok, 1)], xbuf.at[s, pl.ds(r, 1)], gsem.at[s])

    def scatter_copy(dst, s, r):
        return pltpu.make_async_copy(ybuf.at[s, pl.ds(r, 1)], y_hbm.at[pl.ds(dst, 1)], ssem.at[s])

    def weight_copies(e, ws):
        return (pltpu.make_async_copy(wg_hbm.at[layer, e], wg_f.at[ws], wsem.at[ws]),
                pltpu.make_async_copy(wu_hbm.at[layer, e], wu_f.at[ws], wsem.at[ws]),
                pltpu.make_async_copy(wd_hbm.at[layer, e], wd_f.at[ws], wsem.at[ws]))

    def start_gather(b, s):
        for r in range(EXPERT_BLOCK):
            tok = lax.shift_right_logical(assignment(b, r), TOP_K.bit_length() - 1)
            gather_copy(tok, s, r).start()

    def wait_scatter(s):
        for r in range(EXPERT_BLOCK):
            scatter_copy(0, s, r).wait()

    def start_scatter(b, nvalid, s):
        for r in range(EXPERT_BLOCK):
            dst = jnp.where(r < nvalid, assignment(b, r), n_assign + s * EXPERT_BLOCK + r)
            scatter_copy(dst, s, r).start()

    def wait_gather(s):
        for r in range(EXPERT_BLOCK):
            gather_copy(0, s, r).wait()

    @pl.when(i == 0)
    def _():
        ybuf[...] = jnp.zeros(ybuf.shape, ybuf.dtype)
        for s in range(2):
            dump = pltpu.make_async_copy(
                ybuf.at[0], y_hbm.at[pl.ds(n_assign + s * EXPERT_BLOCK, EXPERT_BLOCK)], ssem.at[0])
            dump.start()
            dump.wait()

    @pl.when((i == 0) & (nused > 0))
    def _():
        start_gather(0, 0)
        for cp in weight_copies(meta(0, M_EXPERT), 0):
            cp.start()

    @pl.when(i < nused)
    def _():
        ws = meta(i, M_WSLOT)

        @pl.when(meta(i, M_FIRST) == 1)
        def _():
            for cp in weight_copies(0, ws):
                cp.wait()
            nxt = meta(i, M_NEXT)

            @pl.when(nxt < n_exp)
            def _():
                for cp in weight_copies(nxt, 1 - ws):
                    cp.start()

            wg_b[...] = wg_f[ws].astype(BF16)
            wu_b[...] = wu_f[ws].astype(BF16)
            wd_b[...] = wd_f[ws].astype(BF16)

        @pl.when(i >= 1)
        def _():
            wait_scatter(slot)

        wait_gather(slot)
        start_gather(jnp.minimum(i + 1, nused - 1), 1 - slot)
        prev_valid = jnp.where(i >= 1, meta(jnp.maximum(i - 1, 0), M_VALID), 0)
        start_scatter(jnp.maximum(i - 1, 0), prev_valid, 1 - slot)
        xb = xbuf[slot].astype(BF16)
        hg = _dot(xb, wg_b[...])
        hu = _dot(xb, wu_b[...])
        hid = (hg * jax.nn.sigmoid(hg) * hu).astype(BF16)
        ybuf[slot] = _dot(hid, wd_b[...])

        @pl.when(i == nused - 1)
        def _():
            start_scatter(i, meta(i, M_VALID), slot)
            wait_scatter(1 - slot)
            wait_scatter(slot)
            wait_gather(1 - slot)


def _experts(x1, order, meta, nused, w_gate, w_up, w_down, layer):
    t, d = x1.shape
    n_blocks = meta.shape[0] // BLK_FIELDS
    n_exp, ff = w_gate.shape[1], w_gate.shape[-1]
    n_assign = order.shape[0]
    any_spec = pl.BlockSpec(memory_space=pl.ANY)
    grid_spec = pltpu.PrefetchScalarGridSpec(
        num_scalar_prefetch=3,
        grid=(n_blocks,),
        in_specs=[any_spec] * 4,
        out_specs=any_spec,
        scratch_shapes=[pltpu.VMEM((2, EXPERT_BLOCK, d), F32),
                        pltpu.VMEM((2, EXPERT_BLOCK, d), F32),
                        pltpu.VMEM((2, d, ff), F32), pltpu.VMEM((2, d, ff), F32),
                        pltpu.VMEM((2, ff, d), F32),
                        pltpu.VMEM((d, ff), BF16), pltpu.VMEM((d, ff), BF16),
                        pltpu.VMEM((ff, d), BF16),
                        pltpu.SemaphoreType.DMA((2,)), pltpu.SemaphoreType.DMA((2,)),
                        pltpu.SemaphoreType.DMA((2,))],
    )
    return pl.pallas_call(
        functools.partial(_experts_kernel, layer=layer, n_exp=n_exp),
        grid_spec=grid_spec,
        out_shape=jax.ShapeDtypeStruct((n_assign + 2 * EXPERT_BLOCK, d), F32),
        compiler_params=_cparams(("arbitrary",)),
        name="experts",
    )(order, meta, nused, x1, w_gate, w_up, w_down)


def _final_kernel(y_ref, wcol_ref, x_ref, p_ref, sg_ref, su_ref, sd_ref, pg_ref, pp_ref,
                  g_ref, b_ref, o_ref, *, alpha):
    tm = x_ref.shape[0]
    x = x_ref[...]
    xb = x.astype(BF16)
    hg = _dot(xb, sg_ref[...])
    hu = _dot(xb, su_ref[...])
    shared = _dot((hg * jax.nn.sigmoid(hg) * hu).astype(BF16), sd_ref[...])
    ple = jax.nn.sigmoid(_dot(xb, pg_ref[...])) * _dot(p_ref[...].astype(BF16), pp_ref[...])
    yw = y_ref[...] * wcol_ref[...]
    hi = yw.astype(BF16)
    lo = (yw - hi.astype(F32)).astype(BF16)
    ri = lax.broadcasted_iota(jnp.int32, (tm, tm * TOP_K), 0)
    ci = lax.broadcasted_iota(jnp.int32, (tm, tm * TOP_K), 1)
    sel = (lax.shift_right_logical(ci, TOP_K.bit_length() - 1) == ri).astype(BF16)
    routed = _dot(sel, hi) + _dot(sel, lo)
    z = alpha * x + routed + shared + ple
    o_ref[...] = _layer_norm(z, g_ref[...], b_ref[...], LN_EPS)


def _final(y_assign, wcol, x1, p, sg, su, sd, pg, pp, ln_g, ln_b, alpha, tm=128):
    t, d = x1.shape
    ff = sg.shape[1]
    pd = p.shape[1]
    row = lambda i: (i, 0)
    const = lambda i: (0, 0)
    one = pl.Buffered(1)
    return pl.pallas_call(
        functools.partial(_final_kernel, alpha=alpha),
        grid=(t // tm,),
        in_specs=[pl.BlockSpec((tm * TOP_K, d), row),
                  pl.BlockSpec((tm * TOP_K, 1), row),
                  pl.BlockSpec((tm, d), row),
                  pl.BlockSpec((tm, pd), row),
                  pl.BlockSpec((d, ff), const, pipeline_mode=one),
                  pl.BlockSpec((d, ff), const, pipeline_mode=one),
                  pl.BlockSpec((ff, d), const, pipeline_mode=one),
                  pl.BlockSpec((d, d), const, pipeline_mode=one),
                  pl.BlockSpec((pd, d), const, pipeline_mode=one),
                  pl.BlockSpec((1, d), const), pl.BlockSpec((1, d), const)],
        out_specs=pl.BlockSpec((tm, d), row),
        out_shape=jax.ShapeDtypeStruct((t, d), F32),
        compiler_params=_cparams(("parallel",)),
        name="final",
    )(y_assign, wcol, x1, p, sg, su, sd, pg, pp, ln_g, ln_b)


def _dispatch(idx_t, n_exp):
    k, t = idx_t.shape
    n_assign = t * k
    n_blocks = (n_assign + n_exp * (EXPERT_BLOCK - 1) + EXPERT_BLOCK - 1) // EXPERT_BLOCK
    i32 = jnp.int32
    flat_e = idx_t.T.reshape(-1)
    sorted_e, order = lax.sort_key_val(flat_e, jnp.arange(n_assign, dtype=i32))
    start = jnp.searchsorted(sorted_e, jnp.arange(n_exp, dtype=i32), side="left").astype(i32)
    end = jnp.concatenate([start[1:], jnp.full((1,), n_assign, i32)])
    counts = end - start
    nblk = (counts + EXPERT_BLOCK - 1) // EXPERT_BLOCK
    blk_end = jnp.cumsum(nblk).astype(i32)
    blk_start = blk_end - nblk
    nused = blk_end[-1:]
    b = jnp.arange(n_blocks, dtype=i32)
    e_of_b = jnp.minimum(jnp.searchsorted(blk_end, b, side="right"), n_exp - 1).astype(i32)
    base = start[e_of_b] + (b - blk_start[e_of_b]) * EXPERT_BLOCK
    nvalid = jnp.clip(end[e_of_b] - base, 0, EXPERT_BLOCK)
    first = (b == blk_start[e_of_b]).astype(i32)
    used = counts > 0
    wslot = ((jnp.cumsum(used.astype(i32)) - 1) % 2).astype(i32)
    cand = jnp.where(used, jnp.arange(n_exp, dtype=i32), n_exp)
    nxt = jnp.concatenate([lax.cummin(cand, axis=0, reverse=True)[1:], jnp.full((1,), n_exp, i32)])
    zeros = jnp.zeros_like(b)
    table = jnp.stack([e_of_b, base, nvalid, first, wslot[e_of_b], nxt[e_of_b], zeros, zeros],
                      axis=1).reshape(-1).astype(i32)
    return order, table, nused.astype(i32)


def _pad_rows(w, rows):
    return jnp.pad(w, ((0, rows - w.shape[0]), (0, 0)))


def kernel(x, p, w_in, mu_shift, w0, w2, a0, a2, g2, k_k, k_a, r_k, lnx_g, lnx_b, sgu_ln_g, sgu_ln_b, sgu_w, sgu_b, w_out, ln1_g, ln1_b, router_w, router_bias, exp_gate, exp_up, exp_down, sh_gate, sh_up, sh_down, ple_gate_w, ple_proj, ln2_g, ln2_b):
    batch, seq, d = x.shape
    depth = w_in.shape[0]
    t = batch * seq
    width = d
    n_exp = router_w.shape[-1]
    alpha = float((2 * depth) ** 0.25)
    rwkv_cols = 3 * width + DECAY_LORA + ICLR_LORA + GATE_LORA
    xt = x.reshape(t, d)
    row = lambda vec: vec.reshape(1, -1)

    for i in range(depth):
        wi = w_in[i]
        o = 2 * width
        w_za = wi[:, :o].astype(BF16)
        w_rkv = wi[:, o:o + 3 * width]
        o2 = o + 3 * width
        w_zw = jnp.pad(wi[:, o2:o2 + DECAY_LORA], ((0, 0), (0, LANES - DECAY_LORA)))
        o3 = o2 + DECAY_LORA
        w_zaa = jnp.pad(wi[:, o3:o3 + ICLR_LORA], ((0, 0), (0, LANES - ICLR_LORA)))
        o4 = o3 + ICLR_LORA
        w_zg = wi[:, o4:o4 + GATE_LORA]
        w_proj = jnp.concatenate([w_rkv, w_zw, w_zaa, w_zg], axis=1).astype(BF16)
        w_gates = wi[:, o + rwkv_cols:].astype(BF16)
        ms = mu_shift[i]
        mu = jnp.concatenate([
            ms[:3 * width],
            jnp.pad(ms[3 * width:3 * width + DECAY_LORA], (0, LANES - DECAY_LORA)),
            jnp.pad(ms[3 * width + DECAY_LORA:3 * width + DECAY_LORA + ICLR_LORA],
                    (0, LANES - ICLR_LORA)),
            ms[3 * width + DECAY_LORA + ICLR_LORA:]]).reshape(1, -1)
        w2p = _pad_rows(w2[i], LANES).astype(BF16)
        a2p = _pad_rows(a2[i], LANES).astype(BF16)
        bias_full = jnp.repeat(sgu_b[i].T, width // SGU_GROUPS, axis=1)

        xb = xt.astype(BF16)
        za = _mm_act(xb, w_za, "gelu", BF16, 1024, 512)
        proj = _mm_act(xb, w_proj, None, F32, 1024, 512)
        gates = _mm_act(xb, w_gates, "sigmoid", BF16, 1024, 512)
        out_a = _sgu(za, row(sgu_ln_g[i]), row(sgu_ln_b[i]), sgu_w[i], bias_full)
        r, k, v, lw, a, g = _rwkv_prep(proj, mu, row(w0[i]), w2p, row(a0[i]), a2p,
                                       g2[i].astype(BF16), seq)
        out_b = _rwkv_chunks(r, k, v, lw, a, g, row(k_k[i]), row(k_a[i]), row(r_k[i]),
                             row(lnx_g[i]), row(lnx_b[i]), batch, seq)
        x1 = _mix_ln(out_a, out_b, w_out[i, :width].astype(BF16), w_out[i, width:].astype(BF16),
                     gates, xt, row(ln1_g[i]), row(ln1_b[i]), alpha)

        idx_t, wts_t = _router(x1, router_w[i].T.astype(BF16), router_bias[i].reshape(-1, 1))
        order, table, nused = _dispatch(idx_t, n_exp)
        y_assign = _experts(x1, order, table, nused, exp_gate, exp_up, exp_down, i)
        xt = _final(y_assign, wts_t.T.reshape(-1, 1), x1, p[i].reshape(t, -1),
                    sh_gate[i].astype(BF16), sh_up[i].astype(BF16), sh_down[i].astype(BF16),
                    ple_gate_w[i].astype(BF16), ple_proj[i].astype(BF16),
                    row(ln2_g[i]), row(ln2_b[i]), alpha)
    return xt.reshape(batch, seq, d)
```

```python
import functools
import math

import jax
import jax.numpy as jnp
from jax import lax
from jax.experimental import pallas as pl
from jax.experimental.pallas import tpu as pltpu

F32 = jnp.float32
BF16 = jnp.bfloat16

LANES = 128
VMEM_LIMIT = 56 * 1024 * 1024

SGU_CHUNK = 128
SGU_GROUPS = 16
HEAD = 64
RWKV_CHUNK = 64
HEAD_GROUP_LANES = 256
DECAY_LORA = 96
ICLR_LORA = 96
GATE_LORA = 256
LNX_EPS = 64e-5
LN_EPS = 1e-5
TOP_K = 8
N_GROUPS = 8
TOPK_GROUPS = 4
ROUTED_SCALE = 2.5
EXPERT_BLOCK = 128
INV_SQRT2 = 1.0 / math.sqrt(2.0)


def _cparams(sem, vmem=VMEM_LIMIT):
    return pltpu.CompilerParams(dimension_semantics=sem, vmem_limit_bytes=vmem)


def _dot(a, b):
    return jnp.dot(a, b, preferred_element_type=F32)


def _dot_nt(a, b):
    return lax.dot_general(a, b, (((1,), (1,)), ((), ())), preferred_element_type=F32)


def _layer_norm(z, g, b, eps):
    mu = jnp.mean(z, axis=-1, keepdims=True)
    zc = z - mu
    var = jnp.mean(zc * zc, axis=-1, keepdims=True)
    return zc * lax.rsqrt(var + eps) * g + b


def _rows_to_slabs(slab_ref, x, lead=()):
    rows, d = x.shape
    n = d // LANES
    for j in range(n):
        slab_ref[lead + (pl.ds(j, rows, stride=n), slice(None))] = x[:, j * LANES:(j + 1) * LANES]


def _slabs_to_rows(slab_ref, rows, lead=()):
    n = slab_ref.shape[-2] // rows
    return [slab_ref[lead + (pl.ds(j, rows, stride=n), slice(None))] for j in range(n)]


def _mm_act_kernel(x_ref, w_ref, o_ref, *, act):
    acc = _dot(x_ref[...], w_ref[...])
    if act == "gelu":
        acc = 0.5 * acc * (1.0 + lax.erf(acc * INV_SQRT2))
    elif act == "sigmoid":
        acc = jax.nn.sigmoid(acc)
    o_ref[...] = acc.astype(o_ref.dtype)


def _mm_act(x, w, act, out_dtype, tm, tn):
    m, k = x.shape
    n = w.shape[1]
    tm = min(tm, m)
    return pl.pallas_call(
        functools.partial(_mm_act_kernel, act=act),
        grid=(m // tm, n // tn),
        in_specs=[pl.BlockSpec((tm, k), lambda i, j: (i, 0)),
                  pl.BlockSpec((k, tn), lambda i, j: (0, j))],
        out_specs=pl.BlockSpec((tm, tn), lambda i, j: (i, j)),
        out_shape=jax.ShapeDtypeStruct((m, n), out_dtype),
        compiler_params=_cparams(("parallel", "parallel")),
        name="in_proj_" + str(act),
    )(x, w)


def _sgu_kernel(u_ref, v_ref, g_ref, b_ref, w_ref, bias_ref, o_ref):
    v = v_ref[...].astype(F32)
    vn = _layer_norm(v, g_ref[...], b_ref[...], LN_EPS).astype(BF16)
    ri = lax.broadcasted_iota(jnp.int32, (SGU_CHUNK, SGU_CHUNK), 0)
    ci = lax.broadcasted_iota(jnp.int32, (SGU_CHUNK, SGU_CHUNK), 1)
    causal = ri >= ci
    for g in range(SGU_GROUPS):
        sl = slice(g * LANES, (g + 1) * LANES)
        wg = jnp.where(causal, w_ref[g], 0.0).astype(BF16)
        vm = _dot(wg, vn[:, sl]) + bias_ref[:, sl]
        o_ref[:, sl] = (u_ref[:, sl].astype(F32) * vm).astype(o_ref.dtype)


def _sgu(za, ln_g, ln_b, w_s, bias_full):
    t = za.shape[0]
    width = za.shape[1] // 2
    return pl.pallas_call(
        _sgu_kernel,
        grid=(t // SGU_CHUNK,),
        in_specs=[pl.BlockSpec((SGU_CHUNK, width), lambda c: (c, 0)),
                  pl.BlockSpec((SGU_CHUNK, width), lambda c: (c, 1)),
                  pl.BlockSpec((1, width), lambda c: (0, 0)),
                  pl.BlockSpec((1, width), lambda c: (0, 0)),
                  pl.BlockSpec((SGU_GROUPS, SGU_CHUNK, SGU_CHUNK), lambda c: (0, 0, 0)),
                  pl.BlockSpec((SGU_CHUNK, width), lambda c: (0, 0))],
        out_specs=pl.BlockSpec((SGU_CHUNK, width), lambda c: (c, 0)),
        out_shape=jax.ShapeDtypeStruct((t, width), BF16),
        compiler_params=_cparams(("parallel",)),
        name="sgu",
    )(za, za, ln_g, ln_b, w_s, bias_full)


def _rwkv_prep_kernel(p_ref, prev_ref, mu_ref, w0_ref, w2_ref, a0_ref, a2_ref, g2_ref,
                      r_ref, k_ref, v_ref, lw_ref, a_ref, g_ref, *, tiles_per_seq, width):
    tm = p_ref.shape[0]
    first = (pl.program_id(0) % tiles_per_seq) == 0
    row0 = lax.broadcasted_iota(jnp.int32, (tm, 1), 0) == 0

    def shifted(lo, hi):
        p = p_ref[:, lo:hi]
        last = jnp.where(first, 0.0, prev_ref[7:8, lo:hi])
        prev = jnp.where(row0, last, pltpu.roll(p, 1, axis=0))
        return p + (prev - p) * mu_ref[:, lo:hi]

    r_ref[...] = shifted(0, width)
    k_ref[...] = shifted(width, 2 * width)
    v_ref[...] = shifted(2 * width, 3 * width)
    o = 3 * width
    zw = shifted(o, o + LANES)
    za = shifted(o + LANES, o + 2 * LANES)
    zg = shifted(o + 2 * LANES, o + 2 * LANES + GATE_LORA)
    dw = w0_ref[...] + _dot(jnp.tanh(zw).astype(BF16), w2_ref[...])
    w_log = -(jnp.maximum(-dw, 0.0) + jnp.log(1.0 + jnp.exp(-jnp.abs(dw)))) - 0.5
    lw_ref[...] = -jnp.exp(w_log)
    a_ref[...] = jax.nn.sigmoid(a0_ref[...] + _dot(za.astype(BF16), a2_ref[...]))
    g_ref[...] = _dot(jax.nn.sigmoid(zg).astype(BF16), g2_ref[...])


def _rwkv_prep(proj, mu, w0, w2p, a0, a2p, g2, seq, tm=128):
    t, pc = proj.shape
    width = w0.shape[1]
    row = lambda i: (i, 0)
    const = lambda i: (0, 0)
    out = jax.ShapeDtypeStruct((t, width), F32)
    return pl.pallas_call(
        functools.partial(_rwkv_prep_kernel, tiles_per_seq=seq // tm, width=width),
        grid=(t // tm,),
        in_specs=[pl.BlockSpec((tm, pc), row),
                  pl.BlockSpec((8, pc), lambda i: (jnp.maximum(i * (tm // 8) - 1, 0), 0)),
                  pl.BlockSpec((1, pc), const),
                  pl.BlockSpec((1, width), const),
                  pl.BlockSpec((LANES, width), const),
                  pl.BlockSpec((1, width), const),
                  pl.BlockSpec((LANES, width), const),
                  pl.BlockSpec((GATE_LORA, width), const)],
        out_specs=[pl.BlockSpec((tm, width), row)] * 6,
        out_shape=[out] * 6,
        compiler_params=_cparams(("parallel",)),
        name="rwkv_prep",
    )(proj, proj, mu, w0, w2p, a0, a2p, g2)


def _rwkv_chunk_kernel(r_ref, k_ref, v_ref, lw_ref, a_ref, g_ref,
                       kk_ref, ka_ref, rk_ref, lg_ref, lb_ref,
                       o_ref, h_scr):
    c = RWKV_CHUNK
    w = r_ref.shape[1]
    gw = HEAD_GROUP_LANES
    ng = w // gw
    hpg = gw // HEAD

    @pl.when(pl.program_id(1) == 0)
    def _():
        h_scr[...] = jnp.zeros_like(h_scr)

    bi = lax.broadcasted_iota(jnp.int32, (gw, gw), 0) // HEAD
    bj = lax.broadcasted_iota(jnp.int32, (gw, gw), 1) // HEAD
    block_mask = bi == bj
    block_ones = block_mask.astype(BF16)

    def bd(y):
        return jnp.where(block_mask, jnp.concatenate([y] * hpg, axis=0), jnp.zeros((), y.dtype))

    ti = lax.broadcasted_iota(jnp.int32, (c, gw), 0)
    tj = lax.broadcasted_iota(jnp.int32, (c, gw), 1) % HEAD
    strict = ti > tj
    incl = ti >= tj
    eye = ti == tj
    eye_f = eye.astype(F32)
    eye_b = eye.astype(BF16)
    groups = range(ng)
    gsl = [slice(g * gw, (g + 1) * gw) for g in groups]

    def head_sum(x):
        hi = x.astype(BF16)
        lo = (x - hi.astype(F32)).astype(BF16)
        return jnp.concatenate(
            [_dot(hi[:, s], block_ones) + _dot(lo[:, s], block_ones) for s in gsl], axis=1)

    r = r_ref[...]
    k = k_ref[...]
    v = v_ref[...]
    lw = lw_ref[...]
    a = a_ref[...]

    t0 = lax.broadcasted_iota(jnp.int32, (c, c), 0)
    t1 = lax.broadcasted_iota(jnp.int32, (c, c), 1)
    gcum = jnp.dot((t0 >= t1).astype(F32), lw, precision=lax.Precision.HIGHEST,
                   preferred_element_type=F32)
    g_last = gcum[c - 1:c, :]
    e_g = jnp.exp(gcum)
    e_ng = jnp.exp(-gcum)
    e_gp = jnp.exp(gcum - lw)
    e_h = jnp.exp(g_last - gcum)
    e_last = jnp.exp(g_last)

    kk = k * kk_ref[...]
    kk = kk / jnp.maximum(jnp.sqrt(head_sum(kk * kk)), 1e-12)
    kmod = k * (1.0 + (a - 1.0) * ka_ref[...])
    kb = kk * a
    bonus = head_sum(r * kmod * rk_ref[...]) * v

    at_all = (-kk * e_gp).astype(BF16)
    bt_all = (kb * e_ng).astype(BF16)
    kt_all = (kmod * e_ng).astype(BF16)
    rt_all = r * e_g
    bh_all = (kb * e_h).astype(BF16)
    kh_all = (kmod * e_h).astype(BF16)
    v_all = v.astype(BF16)

    at = [at_all[:, s] for s in gsl]
    rt = [rt_all[:, s] for s in gsl]
    rtb = [x.astype(BF16) for x in rt]
    bt_d = [bd(bt_all[:, s]) for s in gsl]
    kt_d = [bd(kt_all[:, s]) for s in gsl]
    vb_d = [bd(v_all[:, s]) for s in gsl]
    a_ab = [jnp.where(strict, _dot_nt(at[g], bt_d[g]), 0.0) for g in groups]
    a_ak = [jnp.where(strict, _dot_nt(at[g], kt_d[g]), 0.0).astype(BF16) for g in groups]
    m_rb = [jnp.where(incl, _dot_nt(rtb[g], bt_d[g]), 0.0).astype(BF16) for g in groups]
    m_rk = [jnp.where(incl, _dot_nt(rtb[g], kt_d[g]), 0.0).astype(BF16) for g in groups]
    tinv = [eye_f + x for x in a_ab]
    pw = a_ab
    for _ in range(5):
        pwb = [x.astype(BF16) for x in pw]
        pw = [_dot(pwb[g], bd(pwb[g])) for g in groups]
        tinv = [tinv[g] + _dot(tinv[g].astype(BF16), bd(pw[g].astype(BF16))) for g in groups]
    tb = [x.astype(BF16) for x in tinv]
    w1 = [_dot(tb[g], bd(at[g])).astype(BF16) for g in groups]
    av = [_dot(a_ak[g], vb_d[g]).astype(BF16) for g in groups]
    x2 = [_dot(tb[g], bd(av[g])).astype(BF16) for g in groups]
    w1_d = [bd(x) for x in w1]
    x2_d = [bd(x) for x in x2]
    q1 = [(rt[g] + _dot(m_rb[g], w1_d[g])).astype(BF16) for g in groups]
    o2 = [_dot(m_rb[g], x2_d[g]) + _dot(m_rk[g], vb_d[g]) for g in groups]
    bh_t = [_dot_nt(eye_b, bd(bh_all[:, s])).astype(BF16) for s in gsl]
    kh_t = [_dot_nt(eye_b, bd(kh_all[:, s])).astype(BF16) for s in gsl]
    gmat = [(jnp.where(eye, e_last[:, gsl[g]], 0.0) + _dot(bh_t[g], w1_d[g])).astype(BF16)
            for g in groups]
    dmat = [_dot(bh_t[g], x2_d[g]) + _dot(kh_t[g], vb_d[g]) for g in groups]
    h_d = [bd(h_scr[g].astype(BF16)) for g in groups]
    y = jnp.concatenate([_dot(q1[g], h_d[g]) + o2[g] for g in groups], axis=1)
    for g in groups:
        h_scr[g] = _dot(gmat[g], h_d[g]) + dmat[g]

    ym = head_sum(y) * (1.0 / HEAD)
    yc = y - ym
    yv = head_sum(yc * yc) * (1.0 / HEAD)
    yn = yc * lax.rsqrt(yv + LNX_EPS) * lg_ref[...] + lb_ref[...]
    o_ref[...] = ((yn + bonus) * g_ref[...]).astype(o_ref.dtype)


def _rwkv_chunks(r, k, v, lw, a, g, k_k, k_a, r_k, lnx_g, lnx_b, batch, seq):
    t, width = r.shape
    nc = seq // RWKV_CHUNK
    tok = pl.BlockSpec((RWKV_CHUNK, width), lambda b, c: (b * nc + c, 0))
    par = pl.BlockSpec((1, width), lambda b, c: (0, 0))
    return pl.pallas_call(
        _rwkv_chunk_kernel,
        grid=(batch, nc),
        in_specs=[tok] * 6 + [par] * 5,
        out_specs=tok,
        out_shape=jax.ShapeDtypeStruct((t, width), BF16),
        scratch_shapes=[pltpu.VMEM((width // HEAD_GROUP_LANES, HEAD, HEAD_GROUP_LANES), F32)],
        compiler_params=_cparams(("parallel", "arbitrary")),
        name="rwkv_chunks",
    )(r, k, v, lw, a, g, k_k, k_a, r_k, lnx_g, lnx_b)


def _mix_ln_kernel(oa_ref, ob_ref, wa_ref, wb_ref, ga_ref, gb_ref, x_ref, g_ref, b_ref,
                   x1_ref, slab_ref, *, alpha):
    tm, d = x_ref.shape
    ma = _dot(oa_ref[...], wa_ref[...])
    mb = _dot(ob_ref[...], wb_ref[...])
    mix = ga_ref[...].astype(F32) * ma + gb_ref[...].astype(F32) * mb
    z = alpha * x_ref[...] + mix
    x1 = _layer_norm(z, g_ref[...], b_ref[...], LN_EPS)
    x1_ref[...] = x1
    _rows_to_slabs(slab_ref, x1)


def _mix_ln(out_a, out_b, wa, wb, gates, x, ln_g, ln_b, alpha, tm=256):
    t, d = x.shape
    row = lambda i: (i, 0)
    const = lambda i: (0, 0)
    return pl.pallas_call(
        functools.partial(_mix_ln_kernel, alpha=alpha),
        grid=(t // tm,),
        in_specs=[pl.BlockSpec((tm, d), row), pl.BlockSpec((tm, d), row),
                  pl.BlockSpec((d, d), const, pipeline_mode=pl.Buffered(1)),
                  pl.BlockSpec((d, d), const, pipeline_mode=pl.Buffered(1)),
                  pl.BlockSpec((tm, d), row), pl.BlockSpec((tm, d), lambda i: (i, 1)),
                  pl.BlockSpec((tm, d), row),
                  pl.BlockSpec((1, d), const), pl.BlockSpec((1, d), const)],
        out_specs=[pl.BlockSpec((tm, d), row), pl.BlockSpec((tm * (d // LANES), LANES), row)],
        out_shape=[jax.ShapeDtypeStruct((t, d), F32),
                   jax.ShapeDtypeStruct((t * (d // LANES), LANES), F32)],
        compiler_params=_cparams(("parallel",)),
        name="mix_ln",
    )(out_a, out_b, wa, wb, gates, gates, x, ln_g, ln_b)


def _router_kernel(x_ref, wr_ref, bias_ref, idx_ref, wt_ref):
    n_exp = wr_ref.shape[0]
    tm = x_ref.shape[0]
    gsz = n_exp // N_GROUPS
    scores = jax.nn.sigmoid(_dot_nt(wr_ref[...], x_ref[...].astype(BF16)))
    biased = scores + bias_ref[...]
    neg = -jnp.inf

    grp_scores = []
    rows_g = lax.broadcasted_iota(jnp.int32, (gsz, tm), 0)
    for g in range(N_GROUPS):
        blk = biased[g * gsz:(g + 1) * gsz]
        m1 = jnp.max(blk, axis=0, keepdims=True)
        i1 = jnp.min(jnp.where(blk == m1, rows_g, gsz), axis=0, keepdims=True)
        m2 = jnp.max(jnp.where(rows_g == i1, neg, blk), axis=0, keepdims=True)
        grp_scores.append(m1 + m2)
    masked = []
    for g in range(N_GROUPS):
        rank = jnp.zeros((1, tm), jnp.int32)
        for g2 in range(N_GROUPS):
            if g2 == g:
                continue
            ahead = grp_scores[g2] > grp_scores[g]
            if g2 < g:
                ahead = ahead | (grp_scores[g2] == grp_scores[g])
            rank = rank + ahead.astype(jnp.int32)
        blk = biased[g * gsz:(g + 1) * gsz]
        masked.append(jnp.where(rank < TOPK_GROUPS, blk, neg))
    cand = jnp.concatenate(masked, axis=0)

    rows = lax.broadcasted_iota(jnp.int32, (n_exp, tm), 0)
    ids, wts = [], []
    for _ in range(TOP_K):
        m = jnp.max(cand, axis=0, keepdims=True)
        i = jnp.min(jnp.where(cand == m, rows, n_exp), axis=0, keepdims=True)
        hit = rows == i
        ids.append(i)
        wts.append(jnp.sum(jnp.where(hit, scores, 0.0), axis=0, keepdims=True))
        cand = jnp.where(hit, neg, cand)
    wt = jnp.concatenate(wts, axis=0)
    wt = wt / jnp.sum(wt, axis=0, keepdims=True) * ROUTED_SCALE
    idx_ref[...] = jnp.concatenate(ids, axis=0)
    wt_ref[...] = wt


def _router(x1, wr_t, bias_col, tm=512):
    t, d = x1.shape
    n_exp = wr_t.shape[0]
    return pl.pallas_call(
        _router_kernel,
        grid=(t // tm,),
        in_specs=[pl.BlockSpec((tm, d), lambda i: (i, 0)),
                  pl.BlockSpec((n_exp, d), lambda i: (0, 0)),
                  pl.BlockSpec((n_exp, 1), lambda i: (0, 0))],
        out_specs=[pl.BlockSpec((TOP_K, tm), lambda i: (0, i)),
                   pl.BlockSpec((TOP_K, tm), lambda i: (0, i))],
        out_shape=[jax.ShapeDtypeStruct((TOP_K, t), jnp.int32),
                   jax.ShapeDtypeStruct((TOP_K, t), F32)],
        compiler_params=_cparams(("parallel",)),
        name="router",
    )(x1, wr_t, bias_col)


BLK_FIELDS = 8
M_EXPERT, M_BASE, M_VALID, M_FIRST, M_WSLOT, M_NEXT = range(6)


def _experts_kernel(order_ref, meta_ref, nused_ref, x_hbm, wg_hbm, wu_hbm, wd_hbm, y_hbm,
                    xbuf, ybuf, wg_f, wu_f, wd_f, wg_b, wu_b, wd_b, gsem, ssem, wsem,
                    *, layer, n_exp):
    i = pl.program_id(0)
    nused = nused_ref[0]
    n_assign = order_ref.shape[0]
    spr = xbuf.shape[1] // EXPERT_BLOCK
    slot = i % 2

    def meta(b, j):
        return meta_ref[b * BLK_FIELDS + j]

    def assignment(b, r):
        return order_ref[jnp.minimum(meta(b, M_BASE) + r, n_assign - 1)]

    def slab(row):
        return pl.ds(pl.multiple_of(row * spr, spr), spr)

    def gather_copy(tok, s, r):
        return pltpu.make_async_copy(x_hbm.at[slab(tok)], xbuf.at[s, slab(r)], gsem.at[s])

    def scatter_copy(dst, s, r):
        return pltpu.make_async_copy(ybuf.at[s, slab(r)], y_hbm.at[slab(dst)], ssem.at[s])

    def weight_copies(e, ws):
        return (pltpu.make_async_copy(wg_hbm.at[layer, e], wg_f.at[ws], wsem.at[ws]),
                pltpu.make_async_copy(wu_hbm.at[layer, e], wu_f.at[ws], wsem.at[ws]),
                pltpu.make_async_copy(wd_hbm.at[layer, e], wd_f.at[ws], wsem.at[ws]))

    def start_gather(b, s):
        for r in range(EXPERT_BLOCK):
            tok = lax.shift_right_logical(assignment(b, r), TOP_K.bit_length() - 1)
            gather_copy(tok, s, r).start()

    def wait_scatter(s):
        for r in range(EXPERT_BLOCK):
            scatter_copy(0, s, r).wait()

    def start_scatter(b, nvalid, s):
        for r in range(EXPERT_BLOCK):
            dst = jnp.where(r < nvalid, assignment(b, r), n_assign + s * EXPERT_BLOCK + r)
            scatter_copy(dst, s, r).start()

    def wait_gather(s):
        for r in range(EXPERT_BLOCK):
            gather_copy(0, s, r).wait()

    @pl.when(i == 0)
    def _():
        ybuf[...] = jnp.zeros(ybuf.shape, ybuf.dtype)
        for s in range(2):
            dump = pltpu.make_async_copy(
                ybuf.at[0],
                y_hbm.at[pl.ds((n_assign + s * EXPERT_BLOCK) * spr, EXPERT_BLOCK * spr)],
                ssem.at[0])
            dump.start()
            dump.wait()

    @pl.when((i == 0) & (nused > 0))
    def _():
        start_gather(0, 0)
        for cp in weight_copies(meta(0, M_EXPERT), 0):
            cp.start(priority=1)

    @pl.when(i < nused)
    def _():
        ws = meta(i, M_WSLOT)

        @pl.when(meta(i, M_FIRST) == 1)
        def _():
            for cp in weight_copies(0, ws):
                cp.wait()
            nxt = meta(i, M_NEXT)

            @pl.when(nxt < n_exp)
            def _():
                for cp in weight_copies(nxt, 1 - ws):
                    cp.start(priority=1)

            wg_b[...] = wg_f[ws].astype(BF16)
            wu_b[...] = wu_f[ws].astype(BF16)
            wd_b[...] = wd_f[ws].astype(BF16)

        @pl.when(i >= 1)
        def _():
            wait_scatter(slot)

        wait_gather(slot)
        start_gather(jnp.minimum(i + 1, nused - 1), 1 - slot)
        prev_valid = jnp.where(i >= 1, meta(jnp.maximum(i - 1, 0), M_VALID), 0)
        start_scatter(jnp.maximum(i - 1, 0), prev_valid, 1 - slot)
        xb = jnp.concatenate(_slabs_to_rows(xbuf, EXPERT_BLOCK, (slot,)), axis=1).astype(BF16)
        hg = _dot(xb, wg_b[...])
        hu = _dot(xb, wu_b[...])
        hid = (hg * jax.nn.sigmoid(hg) * hu).astype(BF16)
        _rows_to_slabs(ybuf, _dot(hid, wd_b[...]), (slot,))

        @pl.when(i == nused - 1)
        def _():
            start_scatter(i, meta(i, M_VALID), slot)
            wait_scatter(1 - slot)
            wait_scatter(slot)
            wait_gather(1 - slot)


def _experts(x1_slabs, order, meta, nused, w_gate, w_up, w_down, layer):
    d = w_gate.shape[2]
    spr = d // LANES
    n_blocks = meta.shape[0] // BLK_FIELDS
    n_exp, ff = w_gate.shape[1], w_gate.shape[-1]
    n_assign = order.shape[0]
    any_spec = pl.BlockSpec(memory_space=pl.ANY)
    grid_spec = pltpu.PrefetchScalarGridSpec(
        num_scalar_prefetch=3,
        grid=(n_blocks,),
        in_specs=[any_spec] * 4,
        out_specs=any_spec,
        scratch_shapes=[pltpu.VMEM((2, EXPERT_BLOCK * spr, LANES), F32),
                        pltpu.VMEM((2, EXPERT_BLOCK * spr, LANES), F32),
                        pltpu.VMEM((2, d, ff), F32), pltpu.VMEM((2, d, ff), F32),
                        pltpu.VMEM((2, ff, d), F32),
                        pltpu.VMEM((d, ff), BF16), pltpu.VMEM((d, ff), BF16),
                        pltpu.VMEM((ff, d), BF16),
                        pltpu.SemaphoreType.DMA((2,)), pltpu.SemaphoreType.DMA((2,)),
                        pltpu.SemaphoreType.DMA((2,))],
    )
    return pl.pallas_call(
        functools.partial(_experts_kernel, layer=layer, n_exp=n_exp),
        grid_spec=grid_spec,
        out_shape=jax.ShapeDtypeStruct(((n_assign + 2 * EXPERT_BLOCK) * spr, LANES), F32),
        compiler_params=_cparams(("arbitrary",)),
        name="experts",
    )(order, meta, nused, x1_slabs, w_gate, w_up, w_down)


def _final_kernel(y_ref, wcol_ref, x_ref, p_ref, sg_ref, su_ref, sd_ref, pg_ref, pp_ref,
                  g_ref, b_ref, o_ref, *, alpha):
    tm = x_ref.shape[0]
    x = x_ref[...]
    xb = x.astype(BF16)
    hg = _dot(xb, sg_ref[...])
    hu = _dot(xb, su_ref[...])
    shared = _dot((hg * jax.nn.sigmoid(hg) * hu).astype(BF16), sd_ref[...])
    ple = jax.nn.sigmoid(_dot(xb, pg_ref[...])) * _dot(p_ref[...].astype(BF16), pp_ref[...])
    ri = lax.broadcasted_iota(jnp.int32, (tm, tm * TOP_K), 0)
    ci = lax.broadcasted_iota(jnp.int32, (tm, tm * TOP_K), 1)
    sel = (lax.shift_right_logical(ci, TOP_K.bit_length() - 1) == ri).astype(BF16)
    wcol = wcol_ref[...]
    parts = []
    for chunk in _slabs_to_rows(y_ref, tm * TOP_K):
        yw = chunk * wcol
        hi = yw.astype(BF16)
        lo = (yw - hi.astype(F32)).astype(BF16)
        parts.append(_dot(sel, hi) + _dot(sel, lo))
    routed = jnp.concatenate(parts, axis=1)
    z = alpha * x + routed + shared + ple
    o_ref[...] = _layer_norm(z, g_ref[...], b_ref[...], LN_EPS)


def _final(y_assign, wcol, x1, p, sg, su, sd, pg, pp, ln_g, ln_b, alpha, tm=128):
    t, d = x1.shape
    ff = sg.shape[1]
    pd = p.shape[1]
    row = lambda i: (i, 0)
    const = lambda i: (0, 0)
    one = pl.Buffered(1)
    return pl.pallas_call(
        functools.partial(_final_kernel, alpha=alpha),
        grid=(t // tm,),
        in_specs=[pl.BlockSpec((tm * TOP_K * (d // LANES), LANES), row),
                  pl.BlockSpec((tm * TOP_K, 1), row),
                  pl.BlockSpec((tm, d), row),
                  pl.BlockSpec((tm, pd), row),
                  pl.BlockSpec((d, ff), const, pipeline_mode=one),
                  pl.BlockSpec((d, ff), const, pipeline_mode=one),
                  pl.BlockSpec((ff, d), const, pipeline_mode=one),
                  pl.BlockSpec((d, d), const, pipeline_mode=one),
                  pl.BlockSpec((pd, d), const, pipeline_mode=one),
                  pl.BlockSpec((1, d), const), pl.BlockSpec((1, d), const)],
        out_specs=pl.BlockSpec((tm, d), row),
        out_shape=jax.ShapeDtypeStruct((t, d), F32),
        compiler_params=_cparams(("parallel",)),
        name="final",
    )(y_assign, wcol, x1, p, sg, su, sd, pg, pp, ln_g, ln_b)


def _dispatch(idx_t, n_exp):
    k, t = idx_t.shape
    n_assign = t * k
    n_blocks = (n_assign + n_exp * (EXPERT_BLOCK - 1) + EXPERT_BLOCK - 1) // EXPERT_BLOCK
    i32 = jnp.int32
    flat_e = idx_t.T.reshape(-1)
    sorted_e, order = lax.sort_key_val(flat_e, jnp.arange(n_assign, dtype=i32))
    start = jnp.searchsorted(sorted_e, jnp.arange(n_exp, dtype=i32), side="left").astype(i32)
    end = jnp.concatenate([start[1:], jnp.full((1,), n_assign, i32)])
    counts = end - start
    nblk = (counts + EXPERT_BLOCK - 1) // EXPERT_BLOCK
    blk_end = jnp.cumsum(nblk).astype(i32)
    blk_start = blk_end - nblk
    nused = blk_end[-1:]
    b = jnp.arange(n_blocks, dtype=i32)
    e_of_b = jnp.minimum(jnp.searchsorted(blk_end, b, side="right"), n_exp - 1).astype(i32)
    base = start[e_of_b] + (b - blk_start[e_of_b]) * EXPERT_BLOCK
    nvalid = jnp.clip(end[e_of_b] - base, 0, EXPERT_BLOCK)
    first = (b == blk_start[e_of_b]).astype(i32)
    used = counts > 0
    wslot = ((jnp.cumsum(used.astype(i32)) - 1) % 2).astype(i32)
    cand = jnp.where(used, jnp.arange(n_exp, dtype=i32), n_exp)
    nxt = jnp.concatenate([lax.cummin(cand, axis=0, reverse=True)[1:], jnp.full((1,), n_exp, i32)])
    zeros = jnp.zeros_like(b)
    table = jnp.stack([e_of_b, base, nvalid, first, wslot[e_of_b], nxt[e_of_b], zeros, zeros],
                      axis=1).reshape(-1).astype(i32)
    return order, table, nused.astype(i32)


def _pad_rows(w, rows):
    return jnp.pad(w, ((0, rows - w.shape[0]), (0, 0)))


def kernel(x, p, w_in, mu_shift, w0, w2, a0, a2, g2, k_k, k_a, r_k, lnx_g, lnx_b, sgu_ln_g, sgu_ln_b, sgu_w, sgu_b, w_out, ln1_g, ln1_b, router_w, router_bias, exp_gate, exp_up, exp_down, sh_gate, sh_up, sh_down, ple_gate_w, ple_proj, ln2_g, ln2_b):
    batch, seq, d = x.shape
    depth = w_in.shape[0]
    t = batch * seq
    width = d
    n_exp = router_w.shape[-1]
    alpha = float((2 * depth) ** 0.25)
    rwkv_cols = 3 * width + DECAY_LORA + ICLR_LORA + GATE_LORA
    xt = x.reshape(t, d)
    row = lambda vec: vec.reshape(1, -1)

    for i in range(depth):
        wi = w_in[i]
        o = 2 * width
        w_za = wi[:, :o].astype(BF16)
        w_rkv = wi[:, o:o + 3 * width]
        o2 = o + 3 * width
        w_zw = jnp.pad(wi[:, o2:o2 + DECAY_LORA], ((0, 0), (0, LANES - DECAY_LORA)))
        o3 = o2 + DECAY_LORA
        w_zaa = jnp.pad(wi[:, o3:o3 + ICLR_LORA], ((0, 0), (0, LANES - ICLR_LORA)))
        o4 = o3 + ICLR_LORA
        w_zg = wi[:, o4:o4 + GATE_LORA]
        w_proj = jnp.concatenate([w_rkv, w_zw, w_zaa, w_zg], axis=1).astype(BF16)
        w_gates = wi[:, o + rwkv_cols:].astype(BF16)
        ms = mu_shift[i]
        mu = jnp.concatenate([
            ms[:3 * width],
            jnp.pad(ms[3 * width:3 * width + DECAY_LORA], (0, LANES - DECAY_LORA)),
            jnp.pad(ms[3 * width + DECAY_LORA:3 * width + DECAY_LORA + ICLR_LORA],
                    (0, LANES - ICLR_LORA)),
            ms[3 * width + DECAY_LORA + ICLR_LORA:]]).reshape(1, -1)
        w2p = _pad_rows(w2[i], LANES).astype(BF16)
        a2p = _pad_rows(a2[i], LANES).astype(BF16)
        bias_full = jnp.repeat(sgu_b[i].T, width // SGU_GROUPS, axis=1)

        xb = xt.astype(BF16)
        za = _mm_act(xb, w_za, "gelu", BF16, 1024, 512)
        proj = _mm_act(xb, w_proj, None, F32, 1024, 512)
        gates = _mm_act(xb, w_gates, "sigmoid", BF16, 1024, 512)
        out_a = _sgu(za, row(sgu_ln_g[i]), row(sgu_ln_b[i]), sgu_w[i], bias_full)
        r, k, v, lw, a, g = _rwkv_prep(proj, mu, row(w0[i]), w2p, row(a0[i]), a2p,
                                       g2[i].astype(BF16), seq)
        out_b = _rwkv_chunks(r, k, v, lw, a, g, row(k_k[i]), row(k_a[i]), row(r_k[i]),
                             row(lnx_g[i]), row(lnx_b[i]), batch, seq)
        x1, x1_slabs = _mix_ln(out_a, out_b, w_out[i, :width].astype(BF16), w_out[i, width:].astype(BF16),
                     gates, xt, row(ln1_g[i]), row(ln1_b[i]), alpha)

        idx_t, wts_t = _router(x1, router_w[i].T.astype(BF16), router_bias[i].reshape(-1, 1))
        order, table, nused = _dispatch(idx_t, n_exp)
        y_assign = _experts(x1_slabs, order, table, nused, exp_gate, exp_up, exp_down, i)
        xt = _final(y_assign, wts_t.T.reshape(-1, 1), x1, p[i].reshape(t, -1),
                    sh_gate[i].astype(BF16), sh_up[i].astype(BF16), sh_down[i].astype(BF16),
                    ple_gate_w[i].astype(BF16), ple_proj[i].astype(BF16),
                    row(ln2_g[i]), row(ln2_b[i]), alpha)
    return xt.reshape(batch, seq, d)
```

```python
import functools
import math

import jax
import jax.numpy as jnp
from jax import lax
from jax.experimental import pallas as pl
from jax.experimental.pallas import tpu as pltpu

F32 = jnp.float32
BF16 = jnp.bfloat16

LANES = 128
VMEM_LIMIT = 56 * 1024 * 1024

SGU_CHUNK = 128
SGU_GROUPS = 16
HEAD = 64
RWKV_CHUNK = 64
HEAD_GROUP_LANES = 256
DECAY_LORA = 96
ICLR_LORA = 96
GATE_LORA = 256
LNX_EPS = 64e-5
LN_EPS = 1e-5
TOP_K = 8
N_GROUPS = 8
TOPK_GROUPS = 4
ROUTED_SCALE = 2.5
EXPERT_BLOCK = 128
INV_SQRT2 = 1.0 / math.sqrt(2.0)


def _cparams(sem, vmem=VMEM_LIMIT):
    return pltpu.CompilerParams(dimension_semantics=sem, vmem_limit_bytes=vmem)


def _dot(a, b):
    return jnp.dot(a, b, preferred_element_type=F32)


def _dot_nt(a, b):
    return lax.dot_general(a, b, (((1,), (1,)), ((), ())), preferred_element_type=F32)


def _layer_norm(z, g, b, eps):
    mu = jnp.mean(z, axis=-1, keepdims=True)
    zc = z - mu
    var = jnp.mean(zc * zc, axis=-1, keepdims=True)
    return zc * lax.rsqrt(var + eps) * g + b


def _rows_to_slabs(slab_ref, x, lead=()):
    rows, d = x.shape
    n = d // LANES
    for j in range(n):
        slab_ref[lead + (pl.ds(j, rows, stride=n), slice(None))] = x[:, j * LANES:(j + 1) * LANES]


def _slabs_to_rows(slab_ref, rows, lead=()):
    n = slab_ref.shape[-2] // rows
    return [slab_ref[lead + (pl.ds(j, rows, stride=n), slice(None))] for j in range(n)]


def _mm_act_kernel(x_ref, w_ref, o_ref, *, act):
    acc = _dot(x_ref[...], w_ref[...])
    if act == "gelu":
        acc = 0.5 * acc * (1.0 + lax.erf(acc * INV_SQRT2))
    elif act == "sigmoid":
        acc = jax.nn.sigmoid(acc)
    o_ref[...] = acc.astype(o_ref.dtype)


def _mm_act(x, w, act, out_dtype, tm, tn):
    m, k = x.shape
    n = w.shape[1]
    tm = min(tm, m)
    return pl.pallas_call(
        functools.partial(_mm_act_kernel, act=act),
        grid=(m // tm, n // tn),
        in_specs=[pl.BlockSpec((tm, k), lambda i, j: (i, 0)),
                  pl.BlockSpec((k, tn), lambda i, j: (0, j))],
        out_specs=pl.BlockSpec((tm, tn), lambda i, j: (i, j)),
        out_shape=jax.ShapeDtypeStruct((m, n), out_dtype),
        compiler_params=_cparams(("parallel", "parallel")),
        name="in_proj_" + str(act),
    )(x, w)


def _sgu_kernel(u_ref, v_ref, g_ref, b_ref, w_ref, bias_ref, o_ref):
    v = v_ref[...].astype(F32)
    vn = _layer_norm(v, g_ref[...], b_ref[...], LN_EPS).astype(BF16)
    ri = lax.broadcasted_iota(jnp.int32, (SGU_CHUNK, SGU_CHUNK), 0)
    ci = lax.broadcasted_iota(jnp.int32, (SGU_CHUNK, SGU_CHUNK), 1)
    causal = ri >= ci
    for g in range(SGU_GROUPS):
        sl = slice(g * LANES, (g + 1) * LANES)
        wg = jnp.where(causal, w_ref[g], 0.0).astype(BF16)
        vm = _dot(wg, vn[:, sl]) + bias_ref[:, sl]
        o_ref[:, sl] = (u_ref[:, sl].astype(F32) * vm).astype(o_ref.dtype)


def _sgu(za, ln_g, ln_b, w_s, bias_full):
    t = za.shape[0]
    width = za.shape[1] // 2
    return pl.pallas_call(
        _sgu_kernel,
        grid=(t // SGU_CHUNK,),
        in_specs=[pl.BlockSpec((SGU_CHUNK, width), lambda c: (c, 0)),
                  pl.BlockSpec((SGU_CHUNK, width), lambda c: (c, 1)),
                  pl.BlockSpec((1, width), lambda c: (0, 0)),
                  pl.BlockSpec((1, width), lambda c: (0, 0)),
                  pl.BlockSpec((SGU_GROUPS, SGU_CHUNK, SGU_CHUNK), lambda c: (0, 0, 0)),
                  pl.BlockSpec((SGU_CHUNK, width), lambda c: (0, 0))],
        out_specs=pl.BlockSpec((SGU_CHUNK, width), lambda c: (c, 0)),
        out_shape=jax.ShapeDtypeStruct((t, width), BF16),
        compiler_params=_cparams(("parallel",)),
        name="sgu",
    )(za, za, ln_g, ln_b, w_s, bias_full)


def _rwkv_prep_kernel(p_ref, prev_ref, mu_ref, w0_ref, w2_ref, a0_ref, a2_ref, g2_ref,
                      r_ref, k_ref, v_ref, lw_ref, a_ref, g_ref, *, tiles_per_seq, width):
    tm = p_ref.shape[0]
    first = (pl.program_id(0) % tiles_per_seq) == 0
    row0 = lax.broadcasted_iota(jnp.int32, (tm, 1), 0) == 0

    def shifted(lo, hi):
        p = p_ref[:, lo:hi]
        last = jnp.where(first, 0.0, prev_ref[7:8, lo:hi])
        prev = jnp.where(row0, last, pltpu.roll(p, 1, axis=0))
        return p + (prev - p) * mu_ref[:, lo:hi]

    r_ref[...] = shifted(0, width)
    k_ref[...] = shifted(width, 2 * width)
    v_ref[...] = shifted(2 * width, 3 * width)
    o = 3 * width
    zw = shifted(o, o + LANES)
    za = shifted(o + LANES, o + 2 * LANES)
    zg = shifted(o + 2 * LANES, o + 2 * LANES + GATE_LORA)
    dw = w0_ref[...] + _dot(jnp.tanh(zw).astype(BF16), w2_ref[...])
    w_log = -(jnp.maximum(-dw, 0.0) + jnp.log(1.0 + jnp.exp(-jnp.abs(dw)))) - 0.5
    lw_ref[...] = -jnp.exp(w_log)
    a_ref[...] = jax.nn.sigmoid(a0_ref[...] + _dot(za.astype(BF16), a2_ref[...]))
    g_ref[...] = _dot(jax.nn.sigmoid(zg).astype(BF16), g2_ref[...])


def _rwkv_prep(proj, mu, w0, w2p, a0, a2p, g2, seq, tm=128):
    t, pc = proj.shape
    width = w0.shape[1]
    row = lambda i: (i, 0)
    const = lambda i: (0, 0)
    out = jax.ShapeDtypeStruct((t, width), F32)
    return pl.pallas_call(
        functools.partial(_rwkv_prep_kernel, tiles_per_seq=seq // tm, width=width),
        grid=(t // tm,),
        in_specs=[pl.BlockSpec((tm, pc), row),
                  pl.BlockSpec((8, pc), lambda i: (jnp.maximum(i * (tm // 8) - 1, 0), 0)),
                  pl.BlockSpec((1, pc), const),
                  pl.BlockSpec((1, width), const),
                  pl.BlockSpec((LANES, width), const),
                  pl.BlockSpec((1, width), const),
                  pl.BlockSpec((LANES, width), const),
                  pl.BlockSpec((GATE_LORA, width), const)],
        out_specs=[pl.BlockSpec((tm, width), row)] * 6,
        out_shape=[out] * 6,
        compiler_params=_cparams(("parallel",)),
        name="rwkv_prep",
    )(proj, proj, mu, w0, w2p, a0, a2p, g2)


def _rwkv_chunk_kernel(r_ref, k_ref, v_ref, lw_ref, a_ref, g_ref,
                       kk_ref, ka_ref, rk_ref, lg_ref, lb_ref,
                       o_ref, h_scr):
    c = RWKV_CHUNK
    w = r_ref.shape[1]
    gw = HEAD_GROUP_LANES
    ng = w // gw
    hpg = gw // HEAD

    @pl.when(pl.program_id(1) == 0)
    def _():
        h_scr[...] = jnp.zeros_like(h_scr)

    bi = lax.broadcasted_iota(jnp.int32, (gw, gw), 0) // HEAD
    bj = lax.broadcasted_iota(jnp.int32, (gw, gw), 1) // HEAD
    block_mask = bi == bj
    block_ones = block_mask.astype(BF16)

    def bd(y):
        return jnp.where(block_mask, jnp.concatenate([y] * hpg, axis=0), jnp.zeros((), y.dtype))

    ti = lax.broadcasted_iota(jnp.int32, (c, gw), 0)
    tj = lax.broadcasted_iota(jnp.int32, (c, gw), 1) % HEAD
    strict = ti > tj
    incl = ti >= tj
    eye = ti == tj
    eye_f = eye.astype(F32)
    eye_b = eye.astype(BF16)
    groups = range(ng)
    gsl = [slice(g * gw, (g + 1) * gw) for g in groups]

    def head_sum(x):
        hi = x.astype(BF16)
        lo = (x - hi.astype(F32)).astype(BF16)
        return jnp.concatenate(
            [_dot(hi[:, s], block_ones) + _dot(lo[:, s], block_ones) for s in gsl], axis=1)

    r = r_ref[...]
    k = k_ref[...]
    v = v_ref[...]
    lw = lw_ref[...]
    a = a_ref[...]

    t0 = lax.broadcasted_iota(jnp.int32, (c, c), 0)
    t1 = lax.broadcasted_iota(jnp.int32, (c, c), 1)
    gcum = jnp.dot((t0 >= t1).astype(F32), lw, precision=lax.Precision.HIGHEST,
                   preferred_element_type=F32)
    g_last = gcum[c - 1:c, :]
    e_g = jnp.exp(gcum)
    e_ng = jnp.exp(-gcum)
    e_gp = jnp.exp(gcum - lw)
    e_h = jnp.exp(g_last - gcum)
    e_last = jnp.exp(g_last)

    kk = k * kk_ref[...]
    kk = kk / jnp.maximum(jnp.sqrt(head_sum(kk * kk)), 1e-12)
    kmod = k * (1.0 + (a - 1.0) * ka_ref[...])
    kb = kk * a
    bonus = head_sum(r * kmod * rk_ref[...]) * v

    at_all = (-kk * e_gp).astype(BF16)
    bt_all = (kb * e_ng).astype(BF16)
    kt_all = (kmod * e_ng).astype(BF16)
    rt_all = r * e_g
    bh_all = (kb * e_h).astype(BF16)
    kh_all = (kmod * e_h).astype(BF16)
    v_all = v.astype(BF16)

    at = [at_all[:, s] for s in gsl]
    rt = [rt_all[:, s] for s in gsl]
    rtb = [x.astype(BF16) for x in rt]
    bt_d = [bd(bt_all[:, s]) for s in gsl]
    kt_d = [bd(kt_all[:, s]) for s in gsl]
    vb_d = [bd(v_all[:, s]) for s in gsl]
    at_rt = [jnp.concatenate([at[g], rtb[g]], axis=0) for g in groups]
    ab_rb = [_dot_nt(at_rt[g], bt_d[g]) for g in groups]
    ak_rk = [_dot_nt(at_rt[g], kt_d[g]) for g in groups]
    a_ab = [jnp.where(strict, x[:c], 0.0) for x in ab_rb]
    m_rb = [jnp.where(incl, x[c:], 0.0).astype(BF16) for x in ab_rb]
    a_ak = [jnp.where(strict, x[:c], 0.0).astype(BF16) for x in ak_rk]
    m_rk = [jnp.where(incl, x[c:], 0.0).astype(BF16) for x in ak_rk]
    bh_t = [_dot_nt(eye_b, bd(bh_all[:, s])).astype(BF16) for s in gsl]
    kh_t = [_dot_nt(eye_b, bd(kh_all[:, s])).astype(BF16) for s in gsl]
    v_prod = [_dot(jnp.concatenate([a_ak[g], m_rk[g], kh_t[g]], axis=0), vb_d[g]) for g in groups]
    av = [x[:c].astype(BF16) for x in v_prod]
    tinv = [eye_f + x for x in a_ab]
    pw = a_ab
    for _ in range(5):
        pwb = [x.astype(BF16) for x in pw]
        pw = [_dot(pwb[g], bd(pwb[g])) for g in groups]
        tinv = [tinv[g] + _dot(tinv[g].astype(BF16), bd(pw[g].astype(BF16))) for g in groups]
    tb = [x.astype(BF16) for x in tinv]
    w1 = [_dot(tb[g], bd(at[g])).astype(BF16) for g in groups]
    x2 = [_dot(tb[g], bd(av[g])).astype(BF16) for g in groups]
    rb_bh = [jnp.concatenate([m_rb[g], bh_t[g]], axis=0) for g in groups]
    w1_prod = [_dot(rb_bh[g], bd(w1[g])) for g in groups]
    x2_prod = [_dot(rb_bh[g], bd(x2[g])) for g in groups]
    q1 = [(rt[g] + w1_prod[g][:c]).astype(BF16) for g in groups]
    gmat = [(jnp.where(eye, e_last[:, gsl[g]], 0.0) + w1_prod[g][c:]).astype(BF16)
            for g in groups]
    o2 = [x2_prod[g][:c] + v_prod[g][c:2 * c] for g in groups]
    dmat = [x2_prod[g][c:] + v_prod[g][2 * c:] for g in groups]
    h_prod = [_dot(jnp.concatenate([q1[g], gmat[g]], axis=0), bd(h_scr[g].astype(BF16)))
              for g in groups]
    y = jnp.concatenate([h_prod[g][:c] + o2[g] for g in groups], axis=1)
    for g in groups:
        h_scr[g] = h_prod[g][c:] + dmat[g]

    ym = head_sum(y) * (1.0 / HEAD)
    yc = y - ym
    yv = head_sum(yc * yc) * (1.0 / HEAD)
    yn = yc * lax.rsqrt(yv + LNX_EPS) * lg_ref[...] + lb_ref[...]
    o_ref[...] = ((yn + bonus) * g_ref[...]).astype(o_ref.dtype)


def _rwkv_chunks(r, k, v, lw, a, g, k_k, k_a, r_k, lnx_g, lnx_b, batch, seq):
    t, width = r.shape
    nc = seq // RWKV_CHUNK
    tok = pl.BlockSpec((RWKV_CHUNK, width), lambda b, c: (b * nc + c, 0))
    par = pl.BlockSpec((1, width), lambda b, c: (0, 0))
    return pl.pallas_call(
        _rwkv_chunk_kernel,
        grid=(batch, nc),
        in_specs=[tok] * 6 + [par] * 5,
        out_specs=tok,
        out_shape=jax.ShapeDtypeStruct((t, width), BF16),
        scratch_shapes=[pltpu.VMEM((width // HEAD_GROUP_LANES, HEAD, HEAD_GROUP_LANES), F32)],
        compiler_params=_cparams(("parallel", "arbitrary")),
        name="rwkv_chunks",
    )(r, k, v, lw, a, g, k_k, k_a, r_k, lnx_g, lnx_b)


def _mix_ln_kernel(oa_ref, ob_ref, wa_ref, wb_ref, ga_ref, gb_ref, x_ref, g_ref, b_ref,
                   x1_ref, slab_ref, *, alpha):
    tm, d = x_ref.shape
    ma = _dot(oa_ref[...], wa_ref[...])
    mb = _dot(ob_ref[...], wb_ref[...])
    mix = ga_ref[...].astype(F32) * ma + gb_ref[...].astype(F32) * mb
    z = alpha * x_ref[...] + mix
    x1 = _layer_norm(z, g_ref[...], b_ref[...], LN_EPS)
    x1_ref[...] = x1
    _rows_to_slabs(slab_ref, x1)


def _mix_ln(out_a, out_b, wa, wb, gates, x, ln_g, ln_b, alpha, tm=256):
    t, d = x.shape
    row = lambda i: (i, 0)
    const = lambda i: (0, 0)
    return pl.pallas_call(
        functools.partial(_mix_ln_kernel, alpha=alpha),
        grid=(t // tm,),
        in_specs=[pl.BlockSpec((tm, d), row), pl.BlockSpec((tm, d), row),
                  pl.BlockSpec((d, d), const, pipeline_mode=pl.Buffered(1)),
                  pl.BlockSpec((d, d), const, pipeline_mode=pl.Buffered(1)),
                  pl.BlockSpec((tm, d), row), pl.BlockSpec((tm, d), lambda i: (i, 1)),
                  pl.BlockSpec((tm, d), row),
                  pl.BlockSpec((1, d), const), pl.BlockSpec((1, d), const)],
        out_specs=[pl.BlockSpec((tm, d), row), pl.BlockSpec((tm * (d // LANES), LANES), row)],
        out_shape=[jax.ShapeDtypeStruct((t, d), F32),
                   jax.ShapeDtypeStruct((t * (d // LANES), LANES), F32)],
        compiler_params=_cparams(("parallel",)),
        name="mix_ln",
    )(out_a, out_b, wa, wb, gates, gates, x, ln_g, ln_b)


def _router_kernel(x_ref, wr_ref, bias_ref, idx_ref, wt_ref):
    n_exp = wr_ref.shape[0]
    tm = x_ref.shape[0]
    gsz = n_exp // N_GROUPS
    scores = jax.nn.sigmoid(_dot_nt(wr_ref[...], x_ref[...].astype(BF16)))
    biased = scores + bias_ref[...]
    neg = -jnp.inf

    grp_scores = []
    rows_g = lax.broadcasted_iota(jnp.int32, (gsz, tm), 0)
    for g in range(N_GROUPS):
        blk = biased[g * gsz:(g + 1) * gsz]
        m1 = jnp.max(blk, axis=0, keepdims=True)
        i1 = jnp.min(jnp.where(blk == m1, rows_g, gsz), axis=0, keepdims=True)
        m2 = jnp.max(jnp.where(rows_g == i1, neg, blk), axis=0, keepdims=True)
        grp_scores.append(m1 + m2)
    masked = []
    for g in range(N_GROUPS):
        rank = jnp.zeros((1, tm), jnp.int32)
        for g2 in range(N_GROUPS):
            if g2 == g:
                continue
            ahead = grp_scores[g2] > grp_scores[g]
            if g2 < g:
                ahead = ahead | (grp_scores[g2] == grp_scores[g])
            rank = rank + ahead.astype(jnp.int32)
        blk = biased[g * gsz:(g + 1) * gsz]
        masked.append(jnp.where(rank < TOPK_GROUPS, blk, neg))
    cand = jnp.concatenate(masked, axis=0)

    rows = lax.broadcasted_iota(jnp.int32, (n_exp, tm), 0)
    ids, wts = [], []
    for _ in range(TOP_K):
        m = jnp.max(cand, axis=0, keepdims=True)
        i = jnp.min(jnp.where(cand == m, rows, n_exp), axis=0, keepdims=True)
        hit = rows == i
        ids.append(i)
        wts.append(jnp.sum(jnp.where(hit, scores, 0.0), axis=0, keepdims=True))
        cand = jnp.where(hit, neg, cand)
    wt = jnp.concatenate(wts, axis=0)
    wt = wt / jnp.sum(wt, axis=0, keepdims=True) * ROUTED_SCALE
    idx_ref[...] = jnp.concatenate(ids, axis=0)
    wt_ref[...] = wt


def _router(x1, wr_t, bias_col, tm=512):
    t, d = x1.shape
    n_exp = wr_t.shape[0]
    return pl.pallas_call(
        _router_kernel,
        grid=(t // tm,),
        in_specs=[pl.BlockSpec((tm, d), lambda i: (i, 0)),
                  pl.BlockSpec((n_exp, d), lambda i: (0, 0)),
                  pl.BlockSpec((n_exp, 1), lambda i: (0, 0))],
        out_specs=[pl.BlockSpec((TOP_K, tm), lambda i: (0, i)),
                   pl.BlockSpec((TOP_K, tm), lambda i: (0, i))],
        out_shape=[jax.ShapeDtypeStruct((TOP_K, t), jnp.int32),
                   jax.ShapeDtypeStruct((TOP_K, t), F32)],
        compiler_params=_cparams(("parallel",)),
        name="router",
    )(x1, wr_t, bias_col)


BLK_FIELDS = 8
M_EXPERT, M_BASE, M_VALID, M_FIRST, M_WSLOT, M_NEXT = range(6)


def _experts_kernel(order_ref, meta_ref, nused_ref, x_hbm, wg_hbm, wu_hbm, wd_hbm, y_hbm,
                    xbuf, ybuf, wg_f, wu_f, wd_f, wg_b, wu_b, wd_b, gsem, ssem, wsem,
                    *, layer, n_exp):
    i = pl.program_id(0)
    nused = nused_ref[0]
    n_assign = order_ref.shape[0]
    spr = xbuf.shape[1] // EXPERT_BLOCK
    slot = i % 2

    def meta(b, j):
        return meta_ref[b * BLK_FIELDS + j]

    def assignment(b, r):
        return order_ref[jnp.minimum(meta(b, M_BASE) + r, n_assign - 1)]

    def slab(row):
        return pl.ds(pl.multiple_of(row * spr, spr), spr)

    def gather_copy(tok, s, r):
        return pltpu.make_async_copy(x_hbm.at[slab(tok)], xbuf.at[s, slab(r)], gsem.at[s])

    def scatter_copy(dst, s, r):
        return pltpu.make_async_copy(ybuf.at[s, slab(r)], y_hbm.at[slab(dst)], ssem.at[s])

    def weight_copies(e, ws):
        return (pltpu.make_async_copy(wg_hbm.at[layer, e], wg_f.at[ws], wsem.at[ws]),
                pltpu.make_async_copy(wu_hbm.at[layer, e], wu_f.at[ws], wsem.at[ws]),
                pltpu.make_async_copy(wd_hbm.at[layer, e], wd_f.at[ws], wsem.at[ws]))

    def start_gather(b, s):
        for r in range(EXPERT_BLOCK):
            tok = lax.shift_right_logical(assignment(b, r), TOP_K.bit_length() - 1)
            gather_copy(tok, s, r).start()

    def wait_scatter(s):
        for r in range(EXPERT_BLOCK):
            scatter_copy(0, s, r).wait()

    def start_scatter(b, nvalid, s):
        for r in range(EXPERT_BLOCK):
            dst = jnp.where(r < nvalid, assignment(b, r), n_assign + s * EXPERT_BLOCK + r)
            scatter_copy(dst, s, r).start()

    def wait_gather(s):
        for r in range(EXPERT_BLOCK):
            gather_copy(0, s, r).wait()

    @pl.when(i == 0)
    def _():
        ybuf[...] = jnp.zeros(ybuf.shape, ybuf.dtype)
        for s in range(2):
            dump = pltpu.make_async_copy(
                ybuf.at[0],
                y_hbm.at[pl.ds((n_assign + s * EXPERT_BLOCK) * spr, EXPERT_BLOCK * spr)],
                ssem.at[0])
            dump.start()
            dump.wait()

    @pl.when((i == 0) & (nused > 0))
    def _():
        start_gather(0, 0)
        for cp in weight_copies(meta(0, M_EXPERT), 0):
            cp.start(priority=1)

    @pl.when(i < nused)
    def _():
        ws = meta(i, M_WSLOT)

        @pl.when(meta(i, M_FIRST) == 1)
        def _():
            for cp in weight_copies(0, ws):
                cp.wait()
            nxt = meta(i, M_NEXT)

            @pl.when(nxt < n_exp)
            def _():
                for cp in weight_copies(nxt, 1 - ws):
                    cp.start(priority=1)

            wg_b[...] = wg_f[ws].astype(BF16)
            wu_b[...] = wu_f[ws].astype(BF16)
            wd_b[...] = wd_f[ws].astype(BF16)

        @pl.when(i >= 1)
        def _():
            wait_scatter(slot)

        wait_gather(slot)
        start_gather(jnp.minimum(i + 1, nused - 1), 1 - slot)
        prev_valid = jnp.where(i >= 1, meta(jnp.maximum(i - 1, 0), M_VALID), 0)
        start_scatter(jnp.maximum(i - 1, 0), prev_valid, 1 - slot)
        xb = jnp.concatenate(_slabs_to_rows(xbuf, EXPERT_BLOCK, (slot,)), axis=1).astype(BF16)
        hg = _dot(xb, wg_b[...])
        hu = _dot(xb, wu_b[...])
        hid = (hg * jax.nn.sigmoid(hg) * hu).astype(BF16)
        _rows_to_slabs(ybuf, _dot(hid, wd_b[...]), (slot,))

        @pl.when(i == nused - 1)
        def _():
            start_scatter(i, meta(i, M_VALID), slot)
            wait_scatter(1 - slot)
            wait_scatter(slot)
            wait_gather(1 - slot)


def _experts(x1_slabs, order, meta, nused, w_gate, w_up, w_down, layer):
    d = w_gate.shape[2]
    spr = d // LANES
    n_blocks = meta.shape[0] // BLK_FIELDS
    n_exp, ff = w_gate.shape[1], w_gate.shape[-1]
    n_assign = order.shape[0]
    any_spec = pl.BlockSpec(memory_space=pl.ANY)
    grid_spec = pltpu.PrefetchScalarGridSpec(
        num_scalar_prefetch=3,
        grid=(n_blocks,),
        in_specs=[any_spec] * 4,
        out_specs=any_spec,
        scratch_shapes=[pltpu.VMEM((2, EXPERT_BLOCK * spr, LANES), F32),
                        pltpu.VMEM((2, EXPERT_BLOCK * spr, LANES), F32),
                        pltpu.VMEM((2, d, ff), F32), pltpu.VMEM((2, d, ff), F32),
                        pltpu.VMEM((2, ff, d), F32),
                        pltpu.VMEM((d, ff), BF16), pltpu.VMEM((d, ff), BF16),
                        pltpu.VMEM((ff, d), BF16),
                        pltpu.SemaphoreType.DMA((2,)), pltpu.SemaphoreType.DMA((2,)),
                        pltpu.SemaphoreType.DMA((2,))],
    )
    return pl.pallas_call(
        functools.partial(_experts_kernel, layer=layer, n_exp=n_exp),
        grid_spec=grid_spec,
        out_shape=jax.ShapeDtypeStruct(((n_assign + 2 * EXPERT_BLOCK) * spr, LANES), F32),
        compiler_params=_cparams(("arbitrary",)),
        name="experts",
    )(order, meta, nused, x1_slabs, w_gate, w_up, w_down)


def _final_kernel(w_ref, y_ref, x_ref, p_ref, sg_ref, su_ref, sd_ref, pg_ref, pp_ref,
                  g_ref, b_ref, o_ref, routed_scr, *, alpha):
    tm = x_ref.shape[0]
    spr = y_ref.shape[1] // TOP_K
    base = pl.program_id(0) * (tm * TOP_K)

    def combine(t, carry):
        acc = jnp.zeros((spr, LANES), F32)
        for k in range(TOP_K):
            acc = acc + w_ref[base + t * TOP_K + k] * y_ref[t, k * spr:(k + 1) * spr, :]
        routed_scr[pl.ds(pl.multiple_of(t * spr, spr), spr), :] = acc
        return carry

    lax.fori_loop(0, tm, combine, 0, unroll=2)
    routed = jnp.concatenate(_slabs_to_rows(routed_scr, tm), axis=1)
    x = x_ref[...]
    xb = x.astype(BF16)
    hg = _dot(xb, sg_ref[...])
    hu = _dot(xb, su_ref[...])
    shared = _dot((hg * jax.nn.sigmoid(hg) * hu).astype(BF16), sd_ref[...])
    ple = jax.nn.sigmoid(_dot(xb, pg_ref[...])) * _dot(p_ref[...].astype(BF16), pp_ref[...])
    z = alpha * x + routed + shared + ple
    o_ref[...] = _layer_norm(z, g_ref[...], b_ref[...], LN_EPS)


def _final(y_assign, wts_flat, x1, p, sg, su, sd, pg, pp, ln_g, ln_b, alpha, tm=128):
    t, d = x1.shape
    ff = sg.shape[1]
    pd = p.shape[1]
    spr = d // LANES
    y3 = y_assign.reshape(-1, TOP_K * spr, LANES)
    row = lambda i, w: (i, 0)
    const = lambda i, w: (0, 0)
    one = pl.Buffered(1)
    grid_spec = pltpu.PrefetchScalarGridSpec(
        num_scalar_prefetch=1,
        grid=(t // tm,),
        in_specs=[pl.BlockSpec((tm, TOP_K * spr, LANES), lambda i, w: (i, 0, 0)),
                  pl.BlockSpec((tm, d), row),
                  pl.BlockSpec((tm, pd), row),
                  pl.BlockSpec((d, ff), const, pipeline_mode=one),
                  pl.BlockSpec((d, ff), const, pipeline_mode=one),
                  pl.BlockSpec((ff, d), const, pipeline_mode=one),
                  pl.BlockSpec((d, d), const, pipeline_mode=one),
                  pl.BlockSpec((pd, d), const, pipeline_mode=one),
                  pl.BlockSpec((1, d), const), pl.BlockSpec((1, d), const)],
        out_specs=pl.BlockSpec((tm, d), row),
        scratch_shapes=[pltpu.VMEM((tm * spr, LANES), F32)],
    )
    return pl.pallas_call(
        functools.partial(_final_kernel, alpha=alpha),
        grid_spec=grid_spec,
        out_shape=jax.ShapeDtypeStruct((t, d), F32),
        compiler_params=_cparams(("parallel",)),
        name="final",
    )(wts_flat, y3, x1, p, sg, su, sd, pg, pp, ln_g, ln_b)


def _dispatch(idx_t, n_exp):
    k, t = idx_t.shape
    n_assign = t * k
    n_blocks = (n_assign + n_exp * (EXPERT_BLOCK - 1) + EXPERT_BLOCK - 1) // EXPERT_BLOCK
    i32 = jnp.int32
    flat_e = idx_t.T.reshape(-1)
    _, order = lax.sort_key_val(flat_e, jnp.arange(n_assign, dtype=i32))
    e_ids = jnp.arange(n_exp, dtype=i32)
    counts = jnp.sum((flat_e[None, :] == e_ids[:, None]).astype(i32), axis=1)
    end = jnp.cumsum(counts).astype(i32)
    start = end - counts
    nblk = (counts + EXPERT_BLOCK - 1) // EXPERT_BLOCK
    blk_end = jnp.cumsum(nblk).astype(i32)
    blk_start = blk_end - nblk
    nused = blk_end[-1:]
    used = counts > 0
    wslot = (jnp.cumsum(used.astype(i32)) - 1) % 2
    cand = jnp.where(used, e_ids, n_exp)
    nxt = jnp.concatenate([lax.cummin(cand, axis=0, reverse=True)[1:], jnp.full((1,), n_exp, i32)])
    b = jnp.arange(n_blocks, dtype=i32)
    e_of_b = jnp.minimum(jnp.sum((blk_end[None, :] <= b[:, None]).astype(i32), axis=1), n_exp - 1)
    onehot = e_of_b[:, None] == e_ids[None, :]
    pick = lambda v: jnp.sum(jnp.where(onehot, v[None, :].astype(i32), 0), axis=1)
    base = pick(start) + (b - pick(blk_start)) * EXPERT_BLOCK
    nvalid = jnp.clip(pick(end) - base, 0, EXPERT_BLOCK)
    first = (b == pick(blk_start)).astype(i32)
    zeros = jnp.zeros_like(b)
    table = jnp.stack([e_of_b, base, nvalid, first, pick(wslot), pick(nxt), zeros, zeros],
                      axis=1).reshape(-1).astype(i32)
    return order, table, nused.astype(i32)


def _pad_rows(w, rows):
    return jnp.pad(w, ((0, rows - w.shape[0]), (0, 0)))


def kernel(x, p, w_in, mu_shift, w0, w2, a0, a2, g2, k_k, k_a, r_k, lnx_g, lnx_b, sgu_ln_g, sgu_ln_b, sgu_w, sgu_b, w_out, ln1_g, ln1_b, router_w, router_bias, exp_gate, exp_up, exp_down, sh_gate, sh_up, sh_down, ple_gate_w, ple_proj, ln2_g, ln2_b):
    batch, seq, d = x.shape
    depth = w_in.shape[0]
    t = batch * seq
    width = d
    n_exp = router_w.shape[-1]
    alpha = float((2 * depth) ** 0.25)
    rwkv_cols = 3 * width + DECAY_LORA + ICLR_LORA + GATE_LORA
    xt = x.reshape(t, d)
    row = lambda vec: vec.reshape(1, -1)

    for i in range(depth):
        wi = w_in[i]
        o = 2 * width
        w_za = wi[:, :o].astype(BF16)
        w_rkv = wi[:, o:o + 3 * width]
        o2 = o + 3 * width
        w_zw = jnp.pad(wi[:, o2:o2 + DECAY_LORA], ((0, 0), (0, LANES - DECAY_LORA)))
        o3 = o2 + DECAY_LORA
        w_zaa = jnp.pad(wi[:, o3:o3 + ICLR_LORA], ((0, 0), (0, LANES - ICLR_LORA)))
        o4 = o3 + ICLR_LORA
        w_zg = wi[:, o4:o4 + GATE_LORA]
        w_proj = jnp.concatenate([w_rkv, w_zw, w_zaa, w_zg], axis=1).astype(BF16)
        w_gates = wi[:, o + rwkv_cols:].astype(BF16)
        ms = mu_shift[i]
        mu = jnp.concatenate([
            ms[:3 * width],
            jnp.pad(ms[3 * width:3 * width + DECAY_LORA], (0, LANES - DECAY_LORA)),
            jnp.pad(ms[3 * width + DECAY_LORA:3 * width + DECAY_LORA + ICLR_LORA],
                    (0, LANES - ICLR_LORA)),
            ms[3 * width + DECAY_LORA + ICLR_LORA:]]).reshape(1, -1)
        w2p = _pad_rows(w2[i], LANES).astype(BF16)
        a2p = _pad_rows(a2[i], LANES).astype(BF16)
        bias_full = jnp.repeat(sgu_b[i].T, width // SGU_GROUPS, axis=1)

        xb = xt.astype(BF16)
        za = _mm_act(xb, w_za, "gelu", BF16, 1024, 512)
        proj = _mm_act(xb, w_proj, None, F32, 1024, 512)
        gates = _mm_act(xb, w_gates, "sigmoid", BF16, 1024, 512)
        out_a = _sgu(za, row(sgu_ln_g[i]), row(sgu_ln_b[i]), sgu_w[i], bias_full)
        r, k, v, lw, a, g = _rwkv_prep(proj, mu, row(w0[i]), w2p, row(a0[i]), a2p,
                                       g2[i].astype(BF16), seq)
        out_b = _rwkv_chunks(r, k, v, lw, a, g, row(k_k[i]), row(k_a[i]), row(r_k[i]),
                             row(lnx_g[i]), row(lnx_b[i]), batch, seq)
        x1, x1_slabs = _mix_ln(out_a, out_b, w_out[i, :width].astype(BF16), w_out[i, width:].astype(BF16),
                     gates, xt, row(ln1_g[i]), row(ln1_b[i]), alpha)

        idx_t, wts_t = _router(x1, router_w[i].T.astype(BF16), router_bias[i].reshape(-1, 1))
        order, table, nused = _dispatch(idx_t, n_exp)
        y_assign = _experts(x1_slabs, order, table, nused, exp_gate, exp_up, exp_down, i)
        xt = _final(y_assign, wts_t.T.reshape(-1), x1, p[i].reshape(t, -1),
                    sh_gate[i].astype(BF16), sh_up[i].astype(BF16), sh_down[i].astype(BF16),
                    ple_gate_w[i].astype(BF16), ple_proj[i].astype(BF16),
                    row(ln2_g[i]), row(ln2_b[i]), alpha)
    return xt.reshape(batch, seq, d)
```

```python
import functools
import math

import jax
import jax.numpy as jnp
from jax import lax
from jax.experimental import pallas as pl
from jax.experimental.pallas import tpu as pltpu

F32 = jnp.float32
BF16 = jnp.bfloat16

LANES = 128
VMEM_LIMIT = 56 * 1024 * 1024

SGU_CHUNK = 128
SGU_GROUPS = 16
HEAD = 64
RWKV_CHUNK = 64
HEAD_GROUP_LANES = 256
DECAY_LORA = 96
ICLR_LORA = 96
GATE_LORA = 256
LNX_EPS = 64e-5
LN_EPS = 1e-5
TOP_K = 8
N_GROUPS = 8
TOPK_GROUPS = 4
ROUTED_SCALE = 2.5
EXPERT_BLOCK = 128
INV_SQRT2 = 1.0 / math.sqrt(2.0)


def _cparams(sem, vmem=VMEM_LIMIT):
    return pltpu.CompilerParams(dimension_semantics=sem, vmem_limit_bytes=vmem)


def _dot(a, b):
    return jnp.dot(a, b, preferred_element_type=F32)


def _dot_nt(a, b):
    return lax.dot_general(a, b, (((1,), (1,)), ((), ())), preferred_element_type=F32)


def _layer_norm(z, g, b, eps):
    mu = jnp.mean(z, axis=-1, keepdims=True)
    zc = z - mu
    var = jnp.mean(zc * zc, axis=-1, keepdims=True)
    return zc * lax.rsqrt(var + eps) * g + b


def _rows_to_slabs(slab_ref, x, lead=()):
    rows, d = x.shape
    n = d // LANES
    for j in range(n):
        slab_ref[lead + (pl.ds(j, rows, stride=n), slice(None))] = x[:, j * LANES:(j + 1) * LANES]


def _slabs_to_rows(slab_ref, rows, lead=()):
    n = slab_ref.shape[-2] // rows
    return [slab_ref[lead + (pl.ds(j, rows, stride=n), slice(None))] for j in range(n)]


U32 = jnp.uint32
HI_MASK = 0xFFFF0000


def _pack_pair(lo, hi):
    lo_w = lax.bitcast_convert_type(lo.astype(BF16).astype(F32), U32)
    hi_w = lax.bitcast_convert_type(hi.astype(BF16).astype(F32), U32)
    return (hi_w & U32(HI_MASK)) | (lo_w >> 16)


def _unpack_pair(w):
    lo = lax.bitcast_convert_type(w << 16, F32)
    hi = lax.bitcast_convert_type(w & U32(HI_MASK), F32)
    return lo, hi


def _rows_to_packed_slabs(slab_ref, x, lead=()):
    rows, d = x.shape
    half = d // LANES // 2
    for j in range(half):
        slab_ref[lead + (pl.ds(j, rows, stride=half), slice(None))] = _pack_pair(
            x[:, j * LANES:(j + 1) * LANES], x[:, (j + half) * LANES:(j + half + 1) * LANES])


def _packed_slabs_to_rows(slab_ref, rows, lead=()):
    half = slab_ref.shape[-2] // rows
    pairs = [_unpack_pair(slab_ref[lead + (pl.ds(j, rows, stride=half), slice(None))])
             for j in range(half)]
    return [p[0] for p in pairs] + [p[1] for p in pairs]


def _mm_act_kernel(x_ref, w_ref, o_ref, *, act):
    acc = _dot(x_ref[...], w_ref[...])
    if act == "gelu":
        acc = 0.5 * acc * (1.0 + lax.erf(acc * INV_SQRT2))
    elif act == "sigmoid":
        acc = jax.nn.sigmoid(acc)
    o_ref[...] = acc.astype(o_ref.dtype)


def _mm_act(x, w, act, out_dtype, tm, tn):
    m, k = x.shape
    n = w.shape[1]
    tm = min(tm, m)
    return pl.pallas_call(
        functools.partial(_mm_act_kernel, act=act),
        grid=(m // tm, n // tn),
        in_specs=[pl.BlockSpec((tm, k), lambda i, j: (i, 0)),
                  pl.BlockSpec((k, tn), lambda i, j: (0, j))],
        out_specs=pl.BlockSpec((tm, tn), lambda i, j: (i, j)),
        out_shape=jax.ShapeDtypeStruct((m, n), out_dtype),
        compiler_params=_cparams(("parallel", "parallel")),
        name="in_proj_" + str(act),
    )(x, w)


def _sgu_kernel(u_ref, v_ref, g_ref, b_ref, w_ref, bias_ref, o_ref):
    v = v_ref[...].astype(F32)
    vn = _layer_norm(v, g_ref[...], b_ref[...], LN_EPS).astype(BF16)
    ri = lax.broadcasted_iota(jnp.int32, (SGU_CHUNK, SGU_CHUNK), 0)
    ci = lax.broadcasted_iota(jnp.int32, (SGU_CHUNK, SGU_CHUNK), 1)
    causal = ri >= ci
    for g in range(SGU_GROUPS):
        sl = slice(g * LANES, (g + 1) * LANES)
        wg = jnp.where(causal, w_ref[g], 0.0).astype(BF16)
        vm = _dot(wg, vn[:, sl]) + bias_ref[:, sl]
        o_ref[:, sl] = (u_ref[:, sl].astype(F32) * vm).astype(o_ref.dtype)


def _sgu(za, ln_g, ln_b, w_s, bias_full):
    t = za.shape[0]
    width = za.shape[1] // 2
    return pl.pallas_call(
        _sgu_kernel,
        grid=(t // SGU_CHUNK,),
        in_specs=[pl.BlockSpec((SGU_CHUNK, width), lambda c: (c, 0)),
                  pl.BlockSpec((SGU_CHUNK, width), lambda c: (c, 1)),
                  pl.BlockSpec((1, width), lambda c: (0, 0)),
                  pl.BlockSpec((1, width), lambda c: (0, 0)),
                  pl.BlockSpec((SGU_GROUPS, SGU_CHUNK, SGU_CHUNK), lambda c: (0, 0, 0)),
                  pl.BlockSpec((SGU_CHUNK, width), lambda c: (0, 0))],
        out_specs=pl.BlockSpec((SGU_CHUNK, width), lambda c: (c, 0)),
        out_shape=jax.ShapeDtypeStruct((t, width), BF16),
        compiler_params=_cparams(("parallel",)),
        name="sgu",
    )(za, za, ln_g, ln_b, w_s, bias_full)


def _rwkv_prep_kernel(p_ref, prev_ref, mu_ref, w0_ref, w2_ref, a0_ref, a2_ref, g2_ref,
                      r_ref, k_ref, v_ref, lw_ref, a_ref, g_ref, *, tiles_per_seq, width):
    tm = p_ref.shape[0]
    first = (pl.program_id(0) % tiles_per_seq) == 0
    row0 = lax.broadcasted_iota(jnp.int32, (tm, 1), 0) == 0

    def shifted(lo, hi):
        p = p_ref[:, lo:hi]
        last = jnp.where(first, 0.0, prev_ref[7:8, lo:hi])
        prev = jnp.where(row0, last, pltpu.roll(p, 1, axis=0))
        return p + (prev - p) * mu_ref[:, lo:hi]

    r_ref[...] = shifted(0, width)
    k_ref[...] = shifted(width, 2 * width)
    v_ref[...] = shifted(2 * width, 3 * width)
    o = 3 * width
    zw = shifted(o, o + LANES)
    za = shifted(o + LANES, o + 2 * LANES)
    zg = shifted(o + 2 * LANES, o + 2 * LANES + GATE_LORA)
    dw = w0_ref[...] + _dot(jnp.tanh(zw).astype(BF16), w2_ref[...])
    w_log = -(jnp.maximum(-dw, 0.0) + jnp.log(1.0 + jnp.exp(-jnp.abs(dw)))) - 0.5
    lw_ref[...] = -jnp.exp(w_log)
    a_ref[...] = jax.nn.sigmoid(a0_ref[...] + _dot(za.astype(BF16), a2_ref[...]))
    g_ref[...] = _dot(jax.nn.sigmoid(zg).astype(BF16), g2_ref[...])


def _rwkv_prep(proj, mu, w0, w2p, a0, a2p, g2, seq, tm=128):
    t, pc = proj.shape
    width = w0.shape[1]
    row = lambda i: (i, 0)
    const = lambda i: (0, 0)
    out = jax.ShapeDtypeStruct((t, width), F32)
    return pl.pallas_call(
        functools.partial(_rwkv_prep_kernel, tiles_per_seq=seq // tm, width=width),
        grid=(t // tm,),
        in_specs=[pl.BlockSpec((tm, pc), row),
                  pl.BlockSpec((8, pc), lambda i: (jnp.maximum(i * (tm // 8) - 1, 0), 0)),
                  pl.BlockSpec((1, pc), const),
                  pl.BlockSpec((1, width), const),
                  pl.BlockSpec((LANES, width), const),
                  pl.BlockSpec((1, width), const),
                  pl.BlockSpec((LANES, width), const),
                  pl.BlockSpec((GATE_LORA, width), const)],
        out_specs=[pl.BlockSpec((tm, width), row)] * 6,
        out_shape=[out] * 6,
        compiler_params=_cparams(("parallel",)),
        name="rwkv_prep",
    )(proj, proj, mu, w0, w2p, a0, a2p, g2)


def _rwkv_chunk_kernel(r_ref, k_ref, v_ref, lw_ref, a_ref, g_ref,
                       kk_ref, ka_ref, rk_ref, lg_ref, lb_ref,
                       o_ref, h_scr):
    c = RWKV_CHUNK
    w = r_ref.shape[1]
    gw = HEAD_GROUP_LANES
    ng = w // gw
    hpg = gw // HEAD

    @pl.when(pl.program_id(1) == 0)
    def _():
        h_scr[...] = jnp.zeros_like(h_scr)

    bi = lax.broadcasted_iota(jnp.int32, (gw, gw), 0) // HEAD
    bj = lax.broadcasted_iota(jnp.int32, (gw, gw), 1) // HEAD
    block_mask = bi == bj
    block_ones = block_mask.astype(BF16)

    def bd(y):
        return jnp.where(block_mask, jnp.concatenate([y] * hpg, axis=0), jnp.zeros((), y.dtype))

    ti = lax.broadcasted_iota(jnp.int32, (c, gw), 0)
    tj = lax.broadcasted_iota(jnp.int32, (c, gw), 1) % HEAD
    strict = ti > tj
    incl = ti >= tj
    eye = ti == tj
    eye_f = eye.astype(F32)
    eye_b = eye.astype(BF16)
    groups = range(ng)
    gsl = [slice(g * gw, (g + 1) * gw) for g in groups]

    def head_sum(x):
        hi = x.astype(BF16)
        lo = (x - hi.astype(F32)).astype(BF16)
        return jnp.concatenate(
            [_dot(hi[:, s], block_ones) + _dot(lo[:, s], block_ones) for s in gsl], axis=1)

    r = r_ref[...]
    k = k_ref[...]
    v = v_ref[...]
    lw = lw_ref[...]
    a = a_ref[...]

    t0 = lax.broadcasted_iota(jnp.int32, (c, c), 0)
    t1 = lax.broadcasted_iota(jnp.int32, (c, c), 1)
    tri = (t0 >= t1).astype(BF16)
    lw_hi = lw.astype(BF16)
    lw_rest = lw - lw_hi.astype(F32)
    lw_mid = lw_rest.astype(BF16)
    lw_lo = (lw_rest - lw_mid.astype(F32)).astype(BF16)
    gcum = _dot(tri, lw_hi) + _dot(tri, lw_mid) + _dot(tri, lw_lo)
    g_last = gcum[c - 1:c, :]
    e_g = jnp.exp(gcum)
    e_ng = jnp.exp(-gcum)
    e_gp = jnp.exp(gcum - lw)
    e_h = jnp.exp(g_last - gcum)
    e_last = jnp.exp(g_last)

    kk = k * kk_ref[...]
    kk = kk / jnp.maximum(jnp.sqrt(head_sum(kk * kk)), 1e-12)
    kmod = k * (1.0 + (a - 1.0) * ka_ref[...])
    kb = kk * a
    bonus = head_sum(r * kmod * rk_ref[...]) * v

    at_all = (-kk * e_gp).astype(BF16)
    bt_all = (kb * e_ng).astype(BF16)
    kt_all = (kmod * e_ng).astype(BF16)
    rt_all = r * e_g
    bh_all = (kb * e_h).astype(BF16)
    kh_all = (kmod * e_h).astype(BF16)
    v_all = v.astype(BF16)

    at = [at_all[:, s] for s in gsl]
    rt = [rt_all[:, s] for s in gsl]
    rtb = [x.astype(BF16) for x in rt]
    bt_d = [bd(bt_all[:, s]) for s in gsl]
    kt_d = [bd(kt_all[:, s]) for s in gsl]
    vb_d = [bd(v_all[:, s]) for s in gsl]
    at_rt = [jnp.concatenate([at[g], rtb[g]], axis=0) for g in groups]
    ab_rb = [_dot_nt(at_rt[g], bt_d[g]) for g in groups]
    ak_rk = [_dot_nt(at_rt[g], kt_d[g]) for g in groups]
    a_ab = [jnp.where(strict, x[:c], 0.0) for x in ab_rb]
    m_rb = [jnp.where(incl, x[c:], 0.0).astype(BF16) for x in ab_rb]
    a_ak = [jnp.where(strict, x[:c], 0.0).astype(BF16) for x in ak_rk]
    m_rk = [jnp.where(incl, x[c:], 0.0).astype(BF16) for x in ak_rk]
    bh_t = [_dot_nt(eye_b, bd(bh_all[:, s])).astype(BF16) for s in gsl]
    kh_t = [_dot_nt(eye_b, bd(kh_all[:, s])).astype(BF16) for s in gsl]
    v_prod = [_dot(jnp.concatenate([a_ak[g], m_rk[g], kh_t[g]], axis=0), vb_d[g]) for g in groups]
    av = [x[:c].astype(BF16) for x in v_prod]
    tinv = [eye_f + x for x in a_ab]
    pw = a_ab
    for _ in range(5):
        pwb = [x.astype(BF16) for x in pw]
        pw = [_dot(pwb[g], bd(pwb[g])) for g in groups]
        tinv = [tinv[g] + _dot(tinv[g].astype(BF16), bd(pw[g].astype(BF16))) for g in groups]
    tb = [x.astype(BF16) for x in tinv]
    w1 = [_dot(tb[g], bd(at[g])).astype(BF16) for g in groups]
    x2 = [_dot(tb[g], bd(av[g])).astype(BF16) for g in groups]
    rb_bh = [jnp.concatenate([m_rb[g], bh_t[g]], axis=0) for g in groups]
    w1_prod = [_dot(rb_bh[g], bd(w1[g])) for g in groups]
    x2_prod = [_dot(rb_bh[g], bd(x2[g])) for g in groups]
    q1 = [(rt[g] + w1_prod[g][:c]).astype(BF16) for g in groups]
    gmat = [(jnp.where(eye, e_last[:, gsl[g]], 0.0) + w1_prod[g][c:]).astype(BF16)
            for g in groups]
    o2 = [x2_prod[g][:c] + v_prod[g][c:2 * c] for g in groups]
    dmat = [x2_prod[g][c:] + v_prod[g][2 * c:] for g in groups]
    h_prod = [_dot(jnp.concatenate([q1[g], gmat[g]], axis=0), bd(h_scr[g].astype(BF16)))
              for g in groups]
    y = jnp.concatenate([h_prod[g][:c] + o2[g] for g in groups], axis=1)
    for g in groups:
        h_scr[g] = h_prod[g][c:] + dmat[g]

    ym = head_sum(y) * (1.0 / HEAD)
    yc = y - ym
    yv = head_sum(yc * yc) * (1.0 / HEAD)
    yn = yc * lax.rsqrt(yv + LNX_EPS) * lg_ref[...] + lb_ref[...]
    o_ref[...] = ((yn + bonus) * g_ref[...]).astype(o_ref.dtype)


def _rwkv_chunks(r, k, v, lw, a, g, k_k, k_a, r_k, lnx_g, lnx_b, batch, seq):
    t, width = r.shape
    nc = seq // RWKV_CHUNK
    tok = pl.BlockSpec((RWKV_CHUNK, width), lambda b, c: (b * nc + c, 0))
    par = pl.BlockSpec((1, width), lambda b, c: (0, 0))
    return pl.pallas_call(
        _rwkv_chunk_kernel,
        grid=(batch, nc),
        in_specs=[tok] * 6 + [par] * 5,
        out_specs=tok,
        out_shape=jax.ShapeDtypeStruct((t, width), BF16),
        scratch_shapes=[pltpu.VMEM((width // HEAD_GROUP_LANES, HEAD, HEAD_GROUP_LANES), F32)],
        compiler_params=_cparams(("parallel", "arbitrary")),
        name="rwkv_chunks",
    )(r, k, v, lw, a, g, k_k, k_a, r_k, lnx_g, lnx_b)


def _mix_ln_kernel(oa_ref, ob_ref, wa_ref, wb_ref, ga_ref, gb_ref, x_ref, g_ref, b_ref,
                   x1_ref, slab_ref, *, alpha):
    tm, d = x_ref.shape
    ma = _dot(oa_ref[...], wa_ref[...])
    mb = _dot(ob_ref[...], wb_ref[...])
    mix = ga_ref[...].astype(F32) * ma + gb_ref[...].astype(F32) * mb
    z = alpha * x_ref[...] + mix
    x1 = _layer_norm(z, g_ref[...], b_ref[...], LN_EPS)
    x1_ref[...] = x1
    _rows_to_packed_slabs(slab_ref, x1)


def _mix_ln(out_a, out_b, wa, wb, gates, x, ln_g, ln_b, alpha, tm=256):
    t, d = x.shape
    row = lambda i: (i, 0)
    const = lambda i: (0, 0)
    return pl.pallas_call(
        functools.partial(_mix_ln_kernel, alpha=alpha),
        grid=(t // tm,),
        in_specs=[pl.BlockSpec((tm, d), row), pl.BlockSpec((tm, d), row),
                  pl.BlockSpec((d, d), const, pipeline_mode=pl.Buffered(1)),
                  pl.BlockSpec((d, d), const, pipeline_mode=pl.Buffered(1)),
                  pl.BlockSpec((tm, d), row), pl.BlockSpec((tm, d), lambda i: (i, 1)),
                  pl.BlockSpec((tm, d), row),
                  pl.BlockSpec((1, d), const), pl.BlockSpec((1, d), const)],
        out_specs=[pl.BlockSpec((tm, d), row), pl.BlockSpec((tm * (d // LANES // 2), LANES), row)],
        out_shape=[jax.ShapeDtypeStruct((t, d), F32),
                   jax.ShapeDtypeStruct((t * (d // LANES // 2), LANES), U32)],
        compiler_params=_cparams(("parallel",)),
        name="mix_ln",
    )(out_a, out_b, wa, wb, gates, gates, x, ln_g, ln_b)


def _router_kernel(x_ref, wr_ref, bias_ref, idx_ref, wt_ref):
    n_exp = wr_ref.shape[0]
    tm = x_ref.shape[0]
    gsz = n_exp // N_GROUPS
    scores = jax.nn.sigmoid(_dot_nt(wr_ref[...], x_ref[...].astype(BF16)))
    biased = scores + bias_ref[...]
    neg = -jnp.inf

    grp_scores = []
    rows_g = lax.broadcasted_iota(jnp.int32, (gsz, tm), 0)
    for g in range(N_GROUPS):
        blk = biased[g * gsz:(g + 1) * gsz]
        m1 = jnp.max(blk, axis=0, keepdims=True)
        i1 = jnp.min(jnp.where(blk == m1, rows_g, gsz), axis=0, keepdims=True)
        m2 = jnp.max(jnp.where(rows_g == i1, neg, blk), axis=0, keepdims=True)
        grp_scores.append(m1 + m2)
    masked = []
    for g in range(N_GROUPS):
        rank = jnp.zeros((1, tm), jnp.int32)
        for g2 in range(N_GROUPS):
            if g2 == g:
                continue
            ahead = grp_scores[g2] > grp_scores[g]
            if g2 < g:
                ahead = ahead | (grp_scores[g2] == grp_scores[g])
            rank = rank + ahead.astype(jnp.int32)
        blk = biased[g * gsz:(g + 1) * gsz]
        masked.append(jnp.where(rank < TOPK_GROUPS, blk, neg))
    cand = jnp.concatenate(masked, axis=0)

    rows = lax.broadcasted_iota(jnp.int32, (n_exp, tm), 0)
    ids, wts = [], []
    for _ in range(TOP_K):
        m = jnp.max(cand, axis=0, keepdims=True)
        i = jnp.min(jnp.where(cand == m, rows, n_exp), axis=0, keepdims=True)
        hit = rows == i
        ids.append(i)
        wts.append(jnp.sum(jnp.where(hit, scores, 0.0), axis=0, keepdims=True))
        cand = jnp.where(hit, neg, cand)
    wt = jnp.concatenate(wts, axis=0)
    wt = wt / jnp.sum(wt, axis=0, keepdims=True) * ROUTED_SCALE
    idx_ref[...] = jnp.concatenate(ids, axis=0)
    wt_ref[...] = wt


def _router(x1, wr_t, bias_col, tm=512):
    t, d = x1.shape
    n_exp = wr_t.shape[0]
    return pl.pallas_call(
        _router_kernel,
        grid=(t // tm,),
        in_specs=[pl.BlockSpec((tm, d), lambda i: (i, 0)),
                  pl.BlockSpec((n_exp, d), lambda i: (0, 0)),
                  pl.BlockSpec((n_exp, 1), lambda i: (0, 0))],
        out_specs=[pl.BlockSpec((TOP_K, tm), lambda i: (0, i)),
                   pl.BlockSpec((TOP_K, tm), lambda i: (0, i))],
        out_shape=[jax.ShapeDtypeStruct((TOP_K, t), jnp.int32),
                   jax.ShapeDtypeStruct((TOP_K, t), F32)],
        compiler_params=_cparams(("parallel",)),
        name="router",
    )(x1, wr_t, bias_col)


BLK_FIELDS = 8
M_EXPERT, M_BASE, M_VALID, M_FIRST, M_WSLOT, M_NEXT = range(6)


def _experts_kernel(order_ref, meta_ref, nused_ref, x_hbm, wg_hbm, wu_hbm, wd_hbm, y_hbm,
                    xbuf, ybuf, wg_f, wu_f, wd_f, wg_b, wu_b, wd_b, gsem, ssem, wsem,
                    *, layer, n_exp):
    i = pl.program_id(0)
    nused = nused_ref[0]
    n_assign = order_ref.shape[0]
    spr = xbuf.shape[1] // EXPERT_BLOCK
    slot = i % 2

    def meta(b, j):
        return meta_ref[b * BLK_FIELDS + j]

    def assignment(b, r):
        return order_ref[jnp.minimum(meta(b, M_BASE) + r, n_assign - 1)]

    def slab(row):
        return pl.ds(pl.multiple_of(row * spr, spr), spr)

    def gather_copy(tok, s, r):
        return pltpu.make_async_copy(x_hbm.at[slab(tok)], xbuf.at[s, slab(r)], gsem.at[s])

    def scatter_copy(dst, s, r):
        return pltpu.make_async_copy(ybuf.at[s, slab(r)], y_hbm.at[slab(dst)], ssem.at[s])

    def weight_copies(e, ws):
        return (pltpu.make_async_copy(wg_hbm.at[layer, e], wg_f.at[ws], wsem.at[ws]),
                pltpu.make_async_copy(wu_hbm.at[layer, e], wu_f.at[ws], wsem.at[ws]),
                pltpu.make_async_copy(wd_hbm.at[layer, e], wd_f.at[ws], wsem.at[ws]))

    def start_gather(b, s):
        for r in range(EXPERT_BLOCK):
            tok = lax.shift_right_logical(assignment(b, r), TOP_K.bit_length() - 1)
            gather_copy(tok, s, r).start()

    def wait_scatter(s):
        for r in range(EXPERT_BLOCK):
            scatter_copy(0, s, r).wait()

    def start_scatter(b, nvalid, s):
        for r in range(EXPERT_BLOCK):
            dst = jnp.where(r < nvalid, assignment(b, r), n_assign + s * EXPERT_BLOCK + r)
            scatter_copy(dst, s, r).start()

    def wait_gather(s):
        for r in range(EXPERT_BLOCK):
            gather_copy(0, s, r).wait()

    @pl.when(i == 0)
    def _():
        ybuf[...] = jnp.zeros(ybuf.shape, ybuf.dtype)
        for s in range(2):
            dump = pltpu.make_async_copy(
                ybuf.at[0],
                y_hbm.at[pl.ds((n_assign + s * EXPERT_BLOCK) * spr, EXPERT_BLOCK * spr)],
                ssem.at[0])
            dump.start()
            dump.wait()

    @pl.when((i == 0) & (nused > 0))
    def _():
        start_gather(0, 0)
        for cp in weight_copies(meta(0, M_EXPERT), 0):
            cp.start(priority=1)

    @pl.when(i < nused)
    def _():
        ws = meta(i, M_WSLOT)

        @pl.when(meta(i, M_FIRST) == 1)
        def _():
            for cp in weight_copies(0, ws):
                cp.wait()
            nxt = meta(i, M_NEXT)

            @pl.when(nxt < n_exp)
            def _():
                for cp in weight_copies(nxt, 1 - ws):
                    cp.start(priority=1)

            wg_b[...] = wg_f[ws].astype(BF16)
            wu_b[...] = wu_f[ws].astype(BF16)
            wd_b[...] = wd_f[ws].astype(BF16)

        @pl.when(i >= 1)
        def _():
            wait_scatter(slot)

        wait_gather(slot)
        start_gather(jnp.minimum(i + 1, nused - 1), 1 - slot)
        prev_valid = jnp.where(i >= 1, meta(jnp.maximum(i - 1, 0), M_VALID), 0)
        start_scatter(jnp.maximum(i - 1, 0), prev_valid, 1 - slot)
        xb = jnp.concatenate(_packed_slabs_to_rows(xbuf, EXPERT_BLOCK, (slot,)),
                             axis=1).astype(BF16)
        hg = _dot(xb, wg_b[...])
        hu = _dot(xb, wu_b[...])
        hid = (hg * jax.nn.sigmoid(hg) * hu).astype(BF16)
        _rows_to_packed_slabs(ybuf, _dot(hid, wd_b[...]), (slot,))

        @pl.when(i == nused - 1)
        def _():
            start_scatter(i, meta(i, M_VALID), slot)
            wait_scatter(1 - slot)
            wait_scatter(slot)
            wait_gather(1 - slot)


def _experts(x1_slabs, order, meta, nused, w_gate, w_up, w_down, layer):
    d = w_gate.shape[2]
    spr = d // LANES // 2
    n_blocks = meta.shape[0] // BLK_FIELDS
    n_exp, ff = w_gate.shape[1], w_gate.shape[-1]
    n_assign = order.shape[0]
    any_spec = pl.BlockSpec(memory_space=pl.ANY)
    grid_spec = pltpu.PrefetchScalarGridSpec(
        num_scalar_prefetch=3,
        grid=(n_blocks,),
        in_specs=[any_spec] * 4,
        out_specs=any_spec,
        scratch_shapes=[pltpu.VMEM((2, EXPERT_BLOCK * spr, LANES), U32),
                        pltpu.VMEM((2, EXPERT_BLOCK * spr, LANES), U32),
                        pltpu.VMEM((2, d, ff), F32), pltpu.VMEM((2, d, ff), F32),
                        pltpu.VMEM((2, ff, d), F32),
                        pltpu.VMEM((d, ff), BF16), pltpu.VMEM((d, ff), BF16),
                        pltpu.VMEM((ff, d), BF16),
                        pltpu.SemaphoreType.DMA((2,)), pltpu.SemaphoreType.DMA((2,)),
                        pltpu.SemaphoreType.DMA((2,))],
    )
    return pl.pallas_call(
        functools.partial(_experts_kernel, layer=layer, n_exp=n_exp),
        grid_spec=grid_spec,
        out_shape=jax.ShapeDtypeStruct(((n_assign + 2 * EXPERT_BLOCK) * spr, LANES), U32),
        compiler_params=_cparams(("arbitrary",)),
        name="experts",
    )(order, meta, nused, x1_slabs, w_gate, w_up, w_down)


def _final_kernel(w_ref, y_ref, x_ref, p_ref, sg_ref, su_ref, sd_ref, pg_ref, pp_ref,
                  g_ref, b_ref, o_ref, routed_scr, *, alpha):
    tm = x_ref.shape[0]
    spr = y_ref.shape[1] // TOP_K
    base = pl.program_id(0) * (tm * TOP_K)

    def combine(t, carry):
        acc_lo = jnp.zeros((spr, LANES), F32)
        acc_hi = jnp.zeros((spr, LANES), F32)
        for k in range(TOP_K):
            w = w_ref[base + t * TOP_K + k]
            lo, hi = _unpack_pair(y_ref[t, k * spr:(k + 1) * spr, :])
            acc_lo = acc_lo + w * lo
            acc_hi = acc_hi + w * hi
        row0 = pl.multiple_of(t * (2 * spr), 2 * spr)
        routed_scr[pl.ds(row0, spr), :] = acc_lo
        routed_scr[pl.ds(row0 + spr, spr), :] = acc_hi
        return carry

    lax.fori_loop(0, tm, combine, 0, unroll=2)
    routed = jnp.concatenate(_slabs_to_rows(routed_scr, tm), axis=1)
    x = x_ref[...]
    xb = x.astype(BF16)
    hg = _dot(xb, sg_ref[...])
    hu = _dot(xb, su_ref[...])
    shared = _dot((hg * jax.nn.sigmoid(hg) * hu).astype(BF16), sd_ref[...])
    ple = jax.nn.sigmoid(_dot(xb, pg_ref[...])) * _dot(p_ref[...].astype(BF16), pp_ref[...])
    z = alpha * x + routed + shared + ple
    o_ref[...] = _layer_norm(z, g_ref[...], b_ref[...], LN_EPS)


def _final(y_assign, wts_flat, x1, p, sg, su, sd, pg, pp, ln_g, ln_b, alpha, tm=128):
    t, d = x1.shape
    ff = sg.shape[1]
    pd = p.shape[1]
    spr = d // LANES // 2
    y3 = y_assign.reshape(-1, TOP_K * spr, LANES)
    row = lambda i, w: (i, 0)
    const = lambda i, w: (0, 0)
    one = pl.Buffered(1)
    grid_spec = pltpu.PrefetchScalarGridSpec(
        num_scalar_prefetch=1,
        grid=(t // tm,),
        in_specs=[pl.BlockSpec((tm, TOP_K * spr, LANES), lambda i, w: (i, 0, 0)),
                  pl.BlockSpec((tm, d), row),
                  pl.BlockSpec((tm, pd), row),
                  pl.BlockSpec((d, ff), const, pipeline_mode=one),
                  pl.BlockSpec((d, ff), const, pipeline_mode=one),
                  pl.BlockSpec((ff, d), const, pipeline_mode=one),
                  pl.BlockSpec((d, d), const, pipeline_mode=one),
                  pl.BlockSpec((pd, d), const, pipeline_mode=one),
                  pl.BlockSpec((1, d), const), pl.BlockSpec((1, d), const)],
        out_specs=pl.BlockSpec((tm, d), row),
        scratch_shapes=[pltpu.VMEM((tm * 2 * spr, LANES), F32)],
    )
    return pl.pallas_call(
        functools.partial(_final_kernel, alpha=alpha),
        grid_spec=grid_spec,
        out_shape=jax.ShapeDtypeStruct((t, d), F32),
        compiler_params=_cparams(("parallel",)),
        name="final",
    )(wts_flat, y3, x1, p, sg, su, sd, pg, pp, ln_g, ln_b)


def _dispatch(idx_t, n_exp):
    k, t = idx_t.shape
    n_assign = t * k
    n_blocks = (n_assign + n_exp * (EXPERT_BLOCK - 1) + EXPERT_BLOCK - 1) // EXPERT_BLOCK
    i32 = jnp.int32
    flat_e = idx_t.T.reshape(-1)
    _, order = lax.sort_key_val(flat_e, jnp.arange(n_assign, dtype=i32))
    e_ids = jnp.arange(n_exp, dtype=i32)
    counts = jnp.sum((flat_e[None, :] == e_ids[:, None]).astype(i32), axis=1)
    end = jnp.cumsum(counts).astype(i32)
    start = end - counts
    nblk = (counts + EXPERT_BLOCK - 1) // EXPERT_BLOCK
    blk_end = jnp.cumsum(nblk).astype(i32)
    blk_start = blk_end - nblk
    nused = blk_end[-1:]
    used = counts > 0
    wslot = (jnp.cumsum(used.astype(i32)) - 1) % 2
    cand = jnp.where(used, e_ids, n_exp)
    nxt = jnp.concatenate([lax.cummin(cand, axis=0, reverse=True)[1:], jnp.full((1,), n_exp, i32)])
    b = jnp.arange(n_blocks, dtype=i32)
    e_of_b = jnp.minimum(jnp.sum((blk_end[None, :] <= b[:, None]).astype(i32), axis=1), n_exp - 1)
    onehot = e_of_b[:, None] == e_ids[None, :]
    pick = lambda v: jnp.sum(jnp.where(onehot, v[None, :].astype(i32), 0), axis=1)
    base = pick(start) + (b - pick(blk_start)) * EXPERT_BLOCK
    nvalid = jnp.clip(pick(end) - base, 0, EXPERT_BLOCK)
    first = (b == pick(blk_start)).astype(i32)
    zeros = jnp.zeros_like(b)
    table = jnp.stack([e_of_b, base, nvalid, first, pick(wslot), pick(nxt), zeros, zeros],
                      axis=1).reshape(-1).astype(i32)
    return order, table, nused.astype(i32)


def _pad_rows(w, rows):
    return jnp.pad(w, ((0, rows - w.shape[0]), (0, 0)))


def kernel(x, p, w_in, mu_shift, w0, w2, a0, a2, g2, k_k, k_a, r_k, lnx_g, lnx_b, sgu_ln_g, sgu_ln_b, sgu_w, sgu_b, w_out, ln1_g, ln1_b, router_w, router_bias, exp_gate, exp_up, exp_down, sh_gate, sh_up, sh_down, ple_gate_w, ple_proj, ln2_g, ln2_b):
    batch, seq, d = x.shape
    depth = w_in.shape[0]
    t = batch * seq
    width = d
    n_exp = router_w.shape[-1]
    alpha = float((2 * depth) ** 0.25)
    rwkv_cols = 3 * width + DECAY_LORA + ICLR_LORA + GATE_LORA
    xt = x.reshape(t, d)
    row = lambda vec: vec.reshape(1, -1)

    for i in range(depth):
        wi = w_in[i]
        o = 2 * width
        w_za = wi[:, :o].astype(BF16)
        w_rkv = wi[:, o:o + 3 * width]
        o2 = o + 3 * width
        w_zw = jnp.pad(wi[:, o2:o2 + DECAY_LORA], ((0, 0), (0, LANES - DECAY_LORA)))
        o3 = o2 + DECAY_LORA
        w_zaa = jnp.pad(wi[:, o3:o3 + ICLR_LORA], ((0, 0), (0, LANES - ICLR_LORA)))
        o4 = o3 + ICLR_LORA
        w_zg = wi[:, o4:o4 + GATE_LORA]
        w_proj = jnp.concatenate([w_rkv, w_zw, w_zaa, w_zg], axis=1).astype(BF16)
        w_gates = wi[:, o + rwkv_cols:].astype(BF16)
        ms = mu_shift[i]
        mu = jnp.concatenate([
            ms[:3 * width],
            jnp.pad(ms[3 * width:3 * width + DECAY_LORA], (0, LANES - DECAY_LORA)),
            jnp.pad(ms[3 * width + DECAY_LORA:3 * width + DECAY_LORA + ICLR_LORA],
                    (0, LANES - ICLR_LORA)),
            ms[3 * width + DECAY_LORA + ICLR_LORA:]]).reshape(1, -1)
        w2p = _pad_rows(w2[i], LANES).astype(BF16)
        a2p = _pad_rows(a2[i], LANES).astype(BF16)
        bias_full = jnp.repeat(sgu_b[i].T, width // SGU_GROUPS, axis=1)

        xb = xt.astype(BF16)
        za = _mm_act(xb, w_za, "gelu", BF16, 1024, 512)
        proj = _mm_act(xb, w_proj, None, F32, 1024, 512)
        gates = _mm_act(xb, w_gates, "sigmoid", BF16, 1024, 512)
        out_a = _sgu(za, row(sgu_ln_g[i]), row(sgu_ln_b[i]), sgu_w[i], bias_full)
        r, k, v, lw, a, g = _rwkv_prep(proj, mu, row(w0[i]), w2p, row(a0[i]), a2p,
                                       g2[i].astype(BF16), seq)
        out_b = _rwkv_chunks(r, k, v, lw, a, g, row(k_k[i]), row(k_a[i]), row(r_k[i]),
                             row(lnx_g[i]), row(lnx_b[i]), batch, seq)
        x1, x1_slabs = _mix_ln(out_a, out_b, w_out[i, :width].astype(BF16), w_out[i, width:].astype(BF16),
                     gates, xt, row(ln1_g[i]), row(ln1_b[i]), alpha)

        idx_t, wts_t = _router(x1, router_w[i].T.astype(BF16), router_bias[i].reshape(-1, 1))
        order, table, nused = _dispatch(idx_t, n_exp)
        y_assign = _experts(x1_slabs, order, table, nused, exp_gate, exp_up, exp_down, i)
        xt = _final(y_assign, wts_t.T.reshape(-1), x1, p[i].reshape(t, -1),
                    sh_gate[i].astype(BF16), sh_up[i].astype(BF16), sh_down[i].astype(BF16),
                    ple_gate_w[i].astype(BF16), ple_proj[i].astype(BF16),
                    row(ln2_g[i]), row(ln2_b[i]), alpha)
    return xt.reshape(batch, seq, d)
```

```python
import functools
import math

import jax
import jax.numpy as jnp
from jax import lax
from jax.experimental import pallas as pl
from jax.experimental.pallas import tpu as pltpu

F32 = jnp.float32
BF16 = jnp.bfloat16

LANES = 128
VMEM_LIMIT = 56 * 1024 * 1024

SGU_CHUNK = 128
SGU_GROUPS = 16
HEAD = 64
RWKV_CHUNK = 64
HEAD_GROUP_LANES = 256
DECAY_LORA = 96
ICLR_LORA = 96
GATE_LORA = 256
LNX_EPS = 64e-5
LN_EPS = 1e-5
TOP_K = 8
N_GROUPS = 8
TOPK_GROUPS = 4
ROUTED_SCALE = 2.5
EXPERT_BLOCK = 128
INV_SQRT2 = 1.0 / math.sqrt(2.0)


def _cparams(sem, vmem=VMEM_LIMIT):
    return pltpu.CompilerParams(dimension_semantics=sem, vmem_limit_bytes=vmem)


def _dot(a, b):
    return jnp.dot(a, b, preferred_element_type=F32)


def _dot_nt(a, b):
    return lax.dot_general(a, b, (((1,), (1,)), ((), ())), preferred_element_type=F32)


def _layer_norm(z, g, b, eps):
    mu = jnp.mean(z, axis=-1, keepdims=True)
    zc = z - mu
    var = jnp.mean(zc * zc, axis=-1, keepdims=True)
    return zc * lax.rsqrt(var + eps) * g + b


def _rows_to_slabs(slab_ref, x, lead=()):
    rows, d = x.shape
    n = d // LANES
    for j in range(n):
        slab_ref[lead + (pl.ds(j, rows, stride=n), slice(None))] = x[:, j * LANES:(j + 1) * LANES]


def _slabs_to_rows(slab_ref, rows, lead=()):
    n = slab_ref.shape[-2] // rows
    return [slab_ref[lead + (pl.ds(j, rows, stride=n), slice(None))] for j in range(n)]


U32 = jnp.uint32
HI_MASK = 0xFFFF0000


def _pack_pair(lo, hi):
    lo_w = lax.bitcast_convert_type(lo.astype(BF16).astype(F32), U32)
    hi_w = lax.bitcast_convert_type(hi.astype(BF16).astype(F32), U32)
    return (hi_w & U32(HI_MASK)) | (lo_w >> 16)


def _unpack_pair(w):
    lo = lax.bitcast_convert_type(w << 16, F32)
    hi = lax.bitcast_convert_type(w & U32(HI_MASK), F32)
    return lo, hi


def _rows_to_packed_slabs(slab_ref, x, lead=()):
    rows, d = x.shape
    half = d // LANES // 2
    for j in range(half):
        slab_ref[lead + (pl.ds(j, rows, stride=half), slice(None))] = _pack_pair(
            x[:, j * LANES:(j + 1) * LANES], x[:, (j + half) * LANES:(j + half + 1) * LANES])


def _packed_slabs_to_rows(slab_ref, rows, lead=()):
    half = slab_ref.shape[-2] // rows
    pairs = [_unpack_pair(slab_ref[lead + (pl.ds(j, rows, stride=half), slice(None))])
             for j in range(half)]
    return [p[0] for p in pairs] + [p[1] for p in pairs]


def _mm_act_kernel(x_ref, w_ref, o_ref, *, act):
    acc = _dot(x_ref[...], w_ref[...])
    if act == "gelu":
        acc = 0.5 * acc * (1.0 + lax.erf(acc * INV_SQRT2))
    elif act == "sigmoid":
        acc = jax.nn.sigmoid(acc)
    o_ref[...] = acc.astype(o_ref.dtype)


def _mm_act(x, w, act, out_dtype, tm, tn):
    m, k = x.shape
    n = w.shape[1]
    tm = min(tm, m)
    return pl.pallas_call(
        functools.partial(_mm_act_kernel, act=act),
        grid=(m // tm, n // tn),
        in_specs=[pl.BlockSpec((tm, k), lambda i, j: (i, 0)),
                  pl.BlockSpec((k, tn), lambda i, j: (0, j))],
        out_specs=pl.BlockSpec((tm, tn), lambda i, j: (i, j)),
        out_shape=jax.ShapeDtypeStruct((m, n), out_dtype),
        compiler_params=_cparams(("parallel", "parallel")),
        name="in_proj_" + str(act),
    )(x, w)


def _sgu_kernel(u_ref, v_ref, g_ref, b_ref, w_ref, bias_ref, o_ref):
    v = v_ref[...].astype(F32)
    vn = _layer_norm(v, g_ref[...], b_ref[...], LN_EPS).astype(BF16)
    ri = lax.broadcasted_iota(jnp.int32, (SGU_CHUNK, SGU_CHUNK), 0)
    ci = lax.broadcasted_iota(jnp.int32, (SGU_CHUNK, SGU_CHUNK), 1)
    causal = ri >= ci
    for g in range(SGU_GROUPS):
        sl = slice(g * LANES, (g + 1) * LANES)
        wg = jnp.where(causal, w_ref[g], 0.0).astype(BF16)
        vm = _dot(wg, vn[:, sl]) + bias_ref[:, sl]
        o_ref[:, sl] = (u_ref[:, sl].astype(F32) * vm).astype(o_ref.dtype)


def _sgu(za, ln_g, ln_b, w_s, bias_full):
    t = za.shape[0]
    width = za.shape[1] // 2
    return pl.pallas_call(
        _sgu_kernel,
        grid=(t // SGU_CHUNK,),
        in_specs=[pl.BlockSpec((SGU_CHUNK, width), lambda c: (c, 0)),
                  pl.BlockSpec((SGU_CHUNK, width), lambda c: (c, 1)),
                  pl.BlockSpec((1, width), lambda c: (0, 0)),
                  pl.BlockSpec((1, width), lambda c: (0, 0)),
                  pl.BlockSpec((SGU_GROUPS, SGU_CHUNK, SGU_CHUNK), lambda c: (0, 0, 0)),
                  pl.BlockSpec((SGU_CHUNK, width), lambda c: (0, 0))],
        out_specs=pl.BlockSpec((SGU_CHUNK, width), lambda c: (c, 0)),
        out_shape=jax.ShapeDtypeStruct((t, width), BF16),
        compiler_params=_cparams(("parallel",)),
        name="sgu",
    )(za, za, ln_g, ln_b, w_s, bias_full)


def _rwkv_token_inputs(p_ref, prev_ref, mu_ref, w0_ref, w2_ref, a0_ref, a2_ref, g2_ref,
                       first, width):
    tm = p_ref.shape[0]
    row0 = lax.broadcasted_iota(jnp.int32, (tm, 1), 0) == 0

    def shifted(lo, hi):
        p = p_ref[:, lo:hi]
        last = jnp.where(first, 0.0, prev_ref[7:8, lo:hi])
        prev = jnp.where(row0, last, pltpu.roll(p, 1, axis=0))
        return p + (prev - p) * mu_ref[:, lo:hi]

    r = shifted(0, width)
    k = shifted(width, 2 * width)
    v = shifted(2 * width, 3 * width)
    o = 3 * width
    zw = shifted(o, o + LANES)
    za = shifted(o + LANES, o + 2 * LANES)
    zg = shifted(o + 2 * LANES, o + 2 * LANES + GATE_LORA)
    dw = w0_ref[...] + _dot(jnp.tanh(zw).astype(BF16), w2_ref[...])
    w_log = -(jnp.maximum(-dw, 0.0) + jnp.log(1.0 + jnp.exp(-jnp.abs(dw)))) - 0.5
    lw = -jnp.exp(w_log)
    a = jax.nn.sigmoid(a0_ref[...] + _dot(za.astype(BF16), a2_ref[...]))
    g = _dot(jax.nn.sigmoid(zg).astype(BF16), g2_ref[...])
    return r, k, v, lw, a, g


def _rwkv_chunk_kernel(p_ref, prev_ref, mu_ref, w0_ref, w2_ref, a0_ref, a2_ref, g2_ref,
                       kk_ref, ka_ref, rk_ref, lg_ref, lb_ref,
                       o_ref, h_scr):
    c = RWKV_CHUNK
    w = o_ref.shape[1]
    gw = HEAD_GROUP_LANES
    ng = w // gw
    hpg = gw // HEAD

    @pl.when(pl.program_id(1) == 0)
    def _():
        h_scr[...] = jnp.zeros_like(h_scr)

    bi = lax.broadcasted_iota(jnp.int32, (gw, gw), 0) // HEAD
    bj = lax.broadcasted_iota(jnp.int32, (gw, gw), 1) // HEAD
    block_mask = bi == bj
    block_ones = block_mask.astype(BF16)

    def bd(y):
        return jnp.where(block_mask, jnp.concatenate([y] * hpg, axis=0), jnp.zeros((), y.dtype))

    ti = lax.broadcasted_iota(jnp.int32, (c, gw), 0)
    tj = lax.broadcasted_iota(jnp.int32, (c, gw), 1) % HEAD
    strict = ti > tj
    incl = ti >= tj
    eye = ti == tj
    eye_f = eye.astype(F32)
    eye_b = eye.astype(BF16)
    groups = range(ng)
    gsl = [slice(g * gw, (g + 1) * gw) for g in groups]

    def head_sum(x):
        hi = x.astype(BF16)
        lo = (x - hi.astype(F32)).astype(BF16)
        return jnp.concatenate(
            [_dot(hi[:, s], block_ones) + _dot(lo[:, s], block_ones) for s in gsl], axis=1)

    r, k, v, lw, a, gate = _rwkv_token_inputs(
        p_ref, prev_ref, mu_ref, w0_ref, w2_ref, a0_ref, a2_ref, g2_ref,
        pl.program_id(1) == 0, w)

    t0 = lax.broadcasted_iota(jnp.int32, (c, c), 0)
    t1 = lax.broadcasted_iota(jnp.int32, (c, c), 1)
    tri = (t0 >= t1).astype(BF16)
    lw_hi = lw.astype(BF16)
    lw_rest = lw - lw_hi.astype(F32)
    lw_mid = lw_rest.astype(BF16)
    lw_lo = (lw_rest - lw_mid.astype(F32)).astype(BF16)
    gcum = _dot(tri, lw_hi) + _dot(tri, lw_mid) + _dot(tri, lw_lo)
    g_last = gcum[c - 1:c, :]
    e_g = jnp.exp(gcum)
    e_ng = jnp.exp(-gcum)
    e_gp = jnp.exp(gcum - lw)
    e_h = jnp.exp(g_last - gcum)
    e_last = jnp.exp(g_last)

    kk = k * kk_ref[...]
    kk = kk / jnp.maximum(jnp.sqrt(head_sum(kk * kk)), 1e-12)
    kmod = k * (1.0 + (a - 1.0) * ka_ref[...])
    kb = kk * a
    bonus = head_sum(r * kmod * rk_ref[...]) * v

    at_all = (-kk * e_gp).astype(BF16)
    bt_all = (kb * e_ng).astype(BF16)
    kt_all = (kmod * e_ng).astype(BF16)
    rt_all = r * e_g
    bh_all = (kb * e_h).astype(BF16)
    kh_all = (kmod * e_h).astype(BF16)
    v_all = v.astype(BF16)

    at = [at_all[:, s] for s in gsl]
    rt = [rt_all[:, s] for s in gsl]
    rtb = [x.astype(BF16) for x in rt]
    bt_d = [bd(bt_all[:, s]) for s in gsl]
    kt_d = [bd(kt_all[:, s]) for s in gsl]
    vb_d = [bd(v_all[:, s]) for s in gsl]
    at_rt = [jnp.concatenate([at[g], rtb[g]], axis=0) for g in groups]
    ab_rb = [_dot_nt(at_rt[g], bt_d[g]) for g in groups]
    ak_rk = [_dot_nt(at_rt[g], kt_d[g]) for g in groups]
    a_ab = [jnp.where(strict, x[:c], 0.0) for x in ab_rb]
    m_rb = [jnp.where(incl, x[c:], 0.0).astype(BF16) for x in ab_rb]
    a_ak = [jnp.where(strict, x[:c], 0.0).astype(BF16) for x in ak_rk]
    m_rk = [jnp.where(incl, x[c:], 0.0).astype(BF16) for x in ak_rk]
    bh_t = [_dot_nt(eye_b, bd(bh_all[:, s])).astype(BF16) for s in gsl]
    kh_t = [_dot_nt(eye_b, bd(kh_all[:, s])).astype(BF16) for s in gsl]
    v_prod = [_dot(jnp.concatenate([a_ak[g], m_rk[g], kh_t[g]], axis=0), vb_d[g]) for g in groups]
    av = [x[:c].astype(BF16) for x in v_prod]
    tinv = [eye_f + x for x in a_ab]
    pw = a_ab
    for _ in range(5):
        pwb = [x.astype(BF16) for x in pw]
        pw = [_dot(pwb[g], bd(pwb[g])) for g in groups]
        tinv = [tinv[g] + _dot(tinv[g].astype(BF16), bd(pw[g].astype(BF16))) for g in groups]
    tb = [x.astype(BF16) for x in tinv]
    w1 = [_dot(tb[g], bd(at[g])).astype(BF16) for g in groups]
    x2 = [_dot(tb[g], bd(av[g])).astype(BF16) for g in groups]
    rb_bh = [jnp.concatenate([m_rb[g], bh_t[g]], axis=0) for g in groups]
    w1_prod = [_dot(rb_bh[g], bd(w1[g])) for g in groups]
    x2_prod = [_dot(rb_bh[g], bd(x2[g])) for g in groups]
    q1 = [(rt[g] + w1_prod[g][:c]).astype(BF16) for g in groups]
    gmat = [(jnp.where(eye, e_last[:, gsl[g]], 0.0) + w1_prod[g][c:]).astype(BF16)
            for g in groups]
    o2 = [x2_prod[g][:c] + v_prod[g][c:2 * c] for g in groups]
    dmat = [x2_prod[g][c:] + v_prod[g][2 * c:] for g in groups]
    h_prod = [_dot(jnp.concatenate([q1[g], gmat[g]], axis=0), bd(h_scr[g].astype(BF16)))
              for g in groups]
    y = jnp.concatenate([h_prod[g][:c] + o2[g] for g in groups], axis=1)
    for g in groups:
        h_scr[g] = h_prod[g][c:] + dmat[g]

    ym = head_sum(y) * (1.0 / HEAD)
    yc = y - ym
    yv = head_sum(yc * yc) * (1.0 / HEAD)
    yn = yc * lax.rsqrt(yv + LNX_EPS) * lg_ref[...] + lb_ref[...]
    o_ref[...] = ((yn + bonus) * gate).astype(o_ref.dtype)


def _rwkv_chunks(proj, mu, w0, w2p, a0, a2p, g2, k_k, k_a, r_k, lnx_g, lnx_b, batch, seq):
    t, pc = proj.shape
    width = w0.shape[1]
    nc = seq // RWKV_CHUNK
    const = lambda b, c: (0, 0)
    par = pl.BlockSpec((1, width), const)
    return pl.pallas_call(
        _rwkv_chunk_kernel,
        grid=(batch, nc),
        in_specs=[pl.BlockSpec((RWKV_CHUNK, pc), lambda b, c: (b * nc + c, 0)),
                  pl.BlockSpec((8, pc), lambda b, c: (
                      jnp.maximum((b * nc + c) * (RWKV_CHUNK // 8) - 1, 0), 0)),
                  pl.BlockSpec((1, pc), const),
                  par, pl.BlockSpec((LANES, width), const),
                  par, pl.BlockSpec((LANES, width), const),
                  pl.BlockSpec((GATE_LORA, width), const)] + [par] * 5,
        out_specs=pl.BlockSpec((RWKV_CHUNK, width), lambda b, c: (b * nc + c, 0)),
        out_shape=jax.ShapeDtypeStruct((t, width), BF16),
        scratch_shapes=[pltpu.VMEM((width // HEAD_GROUP_LANES, HEAD, HEAD_GROUP_LANES), F32)],
        compiler_params=_cparams(("parallel", "arbitrary")),
        name="rwkv_chunks",
    )(proj, proj, mu, w0, w2p, a0, a2p, g2, k_k, k_a, r_k, lnx_g, lnx_b)


def _mix_ln_kernel(oa_ref, ob_ref, wa_ref, wb_ref, ga_ref, gb_ref, x_ref, g_ref, b_ref,
                   x1_ref, slab_ref, *, alpha):
    tm, d = x_ref.shape
    ma = _dot(oa_ref[...], wa_ref[...])
    mb = _dot(ob_ref[...], wb_ref[...])
    mix = ga_ref[...].astype(F32) * ma + gb_ref[...].astype(F32) * mb
    z = alpha * x_ref[...] + mix
    x1 = _layer_norm(z, g_ref[...], b_ref[...], LN_EPS)
    x1_ref[...] = x1
    _rows_to_packed_slabs(slab_ref, x1)


def _mix_ln(out_a, out_b, wa, wb, gates, x, ln_g, ln_b, alpha, tm=256):
    t, d = x.shape
    row = lambda i: (i, 0)
    const = lambda i: (0, 0)
    return pl.pallas_call(
        functools.partial(_mix_ln_kernel, alpha=alpha),
        grid=(t // tm,),
        in_specs=[pl.BlockSpec((tm, d), row), pl.BlockSpec((tm, d), row),
                  pl.BlockSpec((d, d), const, pipeline_mode=pl.Buffered(1)),
                  pl.BlockSpec((d, d), const, pipeline_mode=pl.Buffered(1)),
                  pl.BlockSpec((tm, d), row), pl.BlockSpec((tm, d), lambda i: (i, 1)),
                  pl.BlockSpec((tm, d), row),
                  pl.BlockSpec((1, d), const), pl.BlockSpec((1, d), const)],
        out_specs=[pl.BlockSpec((tm, d), row), pl.BlockSpec((tm * (d // LANES // 2), LANES), row)],
        out_shape=[jax.ShapeDtypeStruct((t, d), F32),
                   jax.ShapeDtypeStruct((t * (d // LANES // 2), LANES), U32)],
        compiler_params=_cparams(("parallel",)),
        name="mix_ln",
    )(out_a, out_b, wa, wb, gates, gates, x, ln_g, ln_b)


def _router_kernel(x_ref, wr_ref, bias_ref, idx_ref, wt_ref):
    n_exp = wr_ref.shape[0]
    tm = x_ref.shape[0]
    gsz = n_exp // N_GROUPS
    scores = jax.nn.sigmoid(_dot_nt(wr_ref[...], x_ref[...].astype(BF16)))
    biased = scores + bias_ref[...]
    neg = -jnp.inf

    grp_scores = []
    rows_g = lax.broadcasted_iota(jnp.int32, (gsz, tm), 0)
    for g in range(N_GROUPS):
        blk = biased[g * gsz:(g + 1) * gsz]
        m1 = jnp.max(blk, axis=0, keepdims=True)
        i1 = jnp.min(jnp.where(blk == m1, rows_g, gsz), axis=0, keepdims=True)
        m2 = jnp.max(jnp.where(rows_g == i1, neg, blk), axis=0, keepdims=True)
        grp_scores.append(m1 + m2)
    masked = []
    for g in range(N_GROUPS):
        rank = jnp.zeros((1, tm), jnp.int32)
        for g2 in range(N_GROUPS):
            if g2 == g:
                continue
            ahead = grp_scores[g2] > grp_scores[g]
            if g2 < g:
                ahead = ahead | (grp_scores[g2] == grp_scores[g])
            rank = rank + ahead.astype(jnp.int32)
        blk = biased[g * gsz:(g + 1) * gsz]
        masked.append(jnp.where(rank < TOPK_GROUPS, blk, neg))
    cand = jnp.concatenate(masked, axis=0)

    rows = lax.broadcasted_iota(jnp.int32, (n_exp, tm), 0)
    ids, wts = [], []
    for _ in range(TOP_K):
        m = jnp.max(cand, axis=0, keepdims=True)
        i = jnp.min(jnp.where(cand == m, rows, n_exp), axis=0, keepdims=True)
        hit = rows == i
        ids.append(i)
        wts.append(jnp.sum(jnp.where(hit, scores, 0.0), axis=0, keepdims=True))
        cand = jnp.where(hit, neg, cand)
    wt = jnp.concatenate(wts, axis=0)
    wt = wt / jnp.sum(wt, axis=0, keepdims=True) * ROUTED_SCALE
    idx_ref[...] = jnp.concatenate(ids, axis=0)
    wt_ref[...] = wt


def _router(x1, wr_t, bias_col, tm=512):
    t, d = x1.shape
    n_exp = wr_t.shape[0]
    return pl.pallas_call(
        _router_kernel,
        grid=(t // tm,),
        in_specs=[pl.BlockSpec((tm, d), lambda i: (i, 0)),
                  pl.BlockSpec((n_exp, d), lambda i: (0, 0)),
                  pl.BlockSpec((n_exp, 1), lambda i: (0, 0))],
        out_specs=[pl.BlockSpec((TOP_K, tm), lambda i: (0, i)),
                   pl.BlockSpec((TOP_K, tm), lambda i: (0, i))],
        out_shape=[jax.ShapeDtypeStruct((TOP_K, t), jnp.int32),
                   jax.ShapeDtypeStruct((TOP_K, t), F32)],
        compiler_params=_cparams(("parallel",)),
        name="router",
    )(x1, wr_t, bias_col)


BLK_FIELDS = 8
M_EXPERT, M_BASE, M_VALID, M_FIRST, M_WSLOT, M_NEXT = range(6)


def _experts_kernel(order_ref, meta_ref, nused_ref, x_hbm, wg_hbm, wu_hbm, wd_hbm, y_hbm,
                    xbuf, ybuf, wg_f, wu_f, wd_f, wg_b, wu_b, wd_b, gsem, ssem, wsem,
                    *, layer, n_exp):
    i = pl.program_id(0)
    nused = nused_ref[0]
    n_assign = order_ref.shape[0]
    spr = xbuf.shape[1] // EXPERT_BLOCK
    slot = i % 2

    def meta(b, j):
        return meta_ref[b * BLK_FIELDS + j]

    def assignment(b, r):
        return order_ref[jnp.minimum(meta(b, M_BASE) + r, n_assign - 1)]

    def slab(row):
        return pl.ds(pl.multiple_of(row * spr, spr), spr)

    def gather_copy(tok, s, r):
        return pltpu.make_async_copy(x_hbm.at[slab(tok)], xbuf.at[s, slab(r)], gsem.at[s])

    def scatter_copy(dst, s, r):
        return pltpu.make_async_copy(ybuf.at[s, slab(r)], y_hbm.at[slab(dst)], ssem.at[s])

    def weight_copies(e, ws):
        return (pltpu.make_async_copy(wg_hbm.at[layer, e], wg_f.at[ws], wsem.at[ws]),
                pltpu.make_async_copy(wu_hbm.at[layer, e], wu_f.at[ws], wsem.at[ws]),
                pltpu.make_async_copy(wd_hbm.at[layer, e], wd_f.at[ws], wsem.at[ws]))

    def start_gather(b, s):
        for r in range(EXPERT_BLOCK):
            tok = lax.shift_right_logical(assignment(b, r), TOP_K.bit_length() - 1)
            gather_copy(tok, s, r).start()

    def wait_scatter(s):
        for r in range(EXPERT_BLOCK):
            scatter_copy(0, s, r).wait()

    def start_scatter(b, nvalid, s):
        for r in range(EXPERT_BLOCK):
            dst = jnp.where(r < nvalid, assignment(b, r), n_assign + s * EXPERT_BLOCK + r)
            scatter_copy(dst, s, r).start()

    def wait_gather(s):
        for r in range(EXPERT_BLOCK):
            gather_copy(0, s, r).wait()

    @pl.when(i == 0)
    def _():
        ybuf[...] = jnp.zeros(ybuf.shape, ybuf.dtype)
        for s in range(2):
            dump = pltpu.make_async_copy(
                ybuf.at[0],
                y_hbm.at[pl.ds((n_assign + s * EXPERT_BLOCK) * spr, EXPERT_BLOCK * spr)],
                ssem.at[0])
            dump.start()
            dump.wait()

    @pl.when((i == 0) & (nused > 0))
    def _():
        start_gather(0, 0)
        for cp in weight_copies(meta(0, M_EXPERT), 0):
            cp.start(priority=1)

    @pl.when(i < nused)
    def _():
        ws = meta(i, M_WSLOT)

        @pl.when(meta(i, M_FIRST) == 1)
        def _():
            for cp in weight_copies(0, ws):
                cp.wait()
            nxt = meta(i, M_NEXT)

            @pl.when(nxt < n_exp)
            def _():
                for cp in weight_copies(nxt, 1 - ws):
                    cp.start(priority=1)

            wg_b[...] = wg_f[ws].astype(BF16)
            wu_b[...] = wu_f[ws].astype(BF16)
            wd_b[...] = wd_f[ws].astype(BF16)

        @pl.when(i >= 1)
        def _():
            wait_scatter(slot)

        wait_gather(slot)
        start_gather(jnp.minimum(i + 1, nused - 1), 1 - slot)
        prev_valid = jnp.where(i >= 1, meta(jnp.maximum(i - 1, 0), M_VALID), 0)
        start_scatter(jnp.maximum(i - 1, 0), prev_valid, 1 - slot)
        xb = jnp.concatenate(_packed_slabs_to_rows(xbuf, EXPERT_BLOCK, (slot,)),
                             axis=1).astype(BF16)
        hg = _dot(xb, wg_b[...])
        hu = _dot(xb, wu_b[...])
        hid = (hg * jax.nn.sigmoid(hg) * hu).astype(BF16)
        _rows_to_packed_slabs(ybuf, _dot(hid, wd_b[...]), (slot,))

        @pl.when(i == nused - 1)
        def _():
            start_scatter(i, meta(i, M_VALID), slot)
            wait_scatter(1 - slot)
            wait_scatter(slot)
            wait_gather(1 - slot)


def _experts(x1_slabs, order, meta, nused, w_gate, w_up, w_down, layer):
    d = w_gate.shape[2]
    spr = d // LANES // 2
    n_blocks = meta.shape[0] // BLK_FIELDS
    n_exp, ff = w_gate.shape[1], w_gate.shape[-1]
    n_assign = order.shape[0]
    any_spec = pl.BlockSpec(memory_space=pl.ANY)
    grid_spec = pltpu.PrefetchScalarGridSpec(
        num_scalar_prefetch=3,
        grid=(n_blocks,),
        in_specs=[any_spec] * 4,
        out_specs=any_spec,
        scratch_shapes=[pltpu.VMEM((2, EXPERT_BLOCK * spr, LANES), U32),
                        pltpu.VMEM((2, EXPERT_BLOCK * spr, LANES), U32),
                        pltpu.VMEM((2, d, ff), F32), pltpu.VMEM((2, d, ff), F32),
                        pltpu.VMEM((2, ff, d), F32),
                        pltpu.VMEM((d, ff), BF16), pltpu.VMEM((d, ff), BF16),
                        pltpu.VMEM((ff, d), BF16),
                        pltpu.SemaphoreType.DMA((2,)), pltpu.SemaphoreType.DMA((2,)),
                        pltpu.SemaphoreType.DMA((2,))],
    )
    return pl.pallas_call(
        functools.partial(_experts_kernel, layer=layer, n_exp=n_exp),
        grid_spec=grid_spec,
        out_shape=jax.ShapeDtypeStruct(((n_assign + 2 * EXPERT_BLOCK) * spr, LANES), U32),
        compiler_params=_cparams(("arbitrary",)),
        name="experts",
    )(order, meta, nused, x1_slabs, w_gate, w_up, w_down)


def _final_kernel(w_ref, y_ref, x_ref, p_ref, sg_ref, su_ref, sd_ref, pg_ref, pp_ref,
                  g_ref, b_ref, o_ref, routed_scr, *, alpha):
    tm = x_ref.shape[0]
    spr = y_ref.shape[1] // TOP_K
    base = pl.program_id(0) * (tm * TOP_K)

    def combine(t, carry):
        acc_lo = jnp.zeros((spr, LANES), F32)
        acc_hi = jnp.zeros((spr, LANES), F32)
        for k in range(TOP_K):
            w = w_ref[base + t * TOP_K + k]
            lo, hi = _unpack_pair(y_ref[t, k * spr:(k + 1) * spr, :])
            acc_lo = acc_lo + w * lo
            acc_hi = acc_hi + w * hi
        row0 = pl.multiple_of(t * (2 * spr), 2 * spr)
        routed_scr[pl.ds(row0, spr), :] = acc_lo
        routed_scr[pl.ds(row0 + spr, spr), :] = acc_hi
        return carry

    lax.fori_loop(0, tm, combine, 0, unroll=2)
    routed = jnp.concatenate(_slabs_to_rows(routed_scr, tm), axis=1)
    x = x_ref[...]
    xb = x.astype(BF16)
    hg = _dot(xb, sg_ref[...])
    hu = _dot(xb, su_ref[...])
    shared = _dot((hg * jax.nn.sigmoid(hg) * hu).astype(BF16), sd_ref[...])
    ple = jax.nn.sigmoid(_dot(xb, pg_ref[...])) * _dot(p_ref[...].astype(BF16), pp_ref[...])
    z = alpha * x + routed + shared + ple
    o_ref[...] = _layer_norm(z, g_ref[...], b_ref[...], LN_EPS)


def _final(y_assign, wts_flat, x1, p, sg, su, sd, pg, pp, ln_g, ln_b, alpha, tm=256):
    t, d = x1.shape
    ff = sg.shape[1]
    pd = p.shape[1]
    spr = d // LANES // 2
    y3 = y_assign.reshape(-1, TOP_K * spr, LANES)
    row = lambda i, w: (i, 0)
    const = lambda i, w: (0, 0)
    one = pl.Buffered(1)
    grid_spec = pltpu.PrefetchScalarGridSpec(
        num_scalar_prefetch=1,
        grid=(t // tm,),
        in_specs=[pl.BlockSpec((tm, TOP_K * spr, LANES), lambda i, w: (i, 0, 0)),
                  pl.BlockSpec((tm, d), row),
                  pl.BlockSpec((tm, pd), row),
                  pl.BlockSpec((d, ff), const, pipeline_mode=one),
                  pl.BlockSpec((d, ff), const, pipeline_mode=one),
                  pl.BlockSpec((ff, d), const, pipeline_mode=one),
                  pl.BlockSpec((d, d), const, pipeline_mode=one),
                  pl.BlockSpec((pd, d), const, pipeline_mode=one),
                  pl.BlockSpec((1, d), const), pl.BlockSpec((1, d), const)],
        out_specs=pl.BlockSpec((tm, d), row),
        scratch_shapes=[pltpu.VMEM((tm * 2 * spr, LANES), F32)],
    )
    return pl.pallas_call(
        functools.partial(_final_kernel, alpha=alpha),
        grid_spec=grid_spec,
        out_shape=jax.ShapeDtypeStruct((t, d), F32),
        compiler_params=_cparams(("parallel",)),
        name="final",
    )(wts_flat, y3, x1, p, sg, su, sd, pg, pp, ln_g, ln_b)


def _dispatch(idx_t, n_exp):
    k, t = idx_t.shape
    n_assign = t * k
    n_blocks = (n_assign + n_exp * (EXPERT_BLOCK - 1) + EXPERT_BLOCK - 1) // EXPERT_BLOCK
    i32 = jnp.int32
    flat_e = idx_t.T.reshape(-1)
    _, order = lax.sort_key_val(flat_e, jnp.arange(n_assign, dtype=i32))
    e_ids = jnp.arange(n_exp, dtype=i32)
    counts = jnp.sum((flat_e[None, :] == e_ids[:, None]).astype(i32), axis=1)
    end = jnp.cumsum(counts).astype(i32)
    start = end - counts
    nblk = (counts + EXPERT_BLOCK - 1) // EXPERT_BLOCK
    blk_end = jnp.cumsum(nblk).astype(i32)
    blk_start = blk_end - nblk
    nused = blk_end[-1:]
    used = counts > 0
    wslot = (jnp.cumsum(used.astype(i32)) - 1) % 2
    cand = jnp.where(used, e_ids, n_exp)
    nxt = jnp.concatenate([lax.cummin(cand, axis=0, reverse=True)[1:], jnp.full((1,), n_exp, i32)])
    b = jnp.arange(n_blocks, dtype=i32)
    e_of_b = jnp.minimum(jnp.sum((blk_end[None, :] <= b[:, None]).astype(i32), axis=1), n_exp - 1)
    onehot = e_of_b[:, None] == e_ids[None, :]
    pick = lambda v: jnp.sum(jnp.where(onehot, v[None, :].astype(i32), 0), axis=1)
    base = pick(start) + (b - pick(blk_start)) * EXPERT_BLOCK
    nvalid = jnp.clip(pick(end) - base, 0, EXPERT_BLOCK)
    first = (b == pick(blk_start)).astype(i32)
    zeros = jnp.zeros_like(b)
    table = jnp.stack([e_of_b, base, nvalid, first, pick(wslot), pick(nxt), zeros, zeros],
                      axis=1).reshape(-1).astype(i32)
    return order, table, nused.astype(i32)


def _pad_rows(w, rows):
    return jnp.pad(w, ((0, rows - w.shape[0]), (0, 0)))


def kernel(x, p, w_in, mu_shift, w0, w2, a0, a2, g2, k_k, k_a, r_k, lnx_g, lnx_b, sgu_ln_g, sgu_ln_b, sgu_w, sgu_b, w_out, ln1_g, ln1_b, router_w, router_bias, exp_gate, exp_up, exp_down, sh_gate, sh_up, sh_down, ple_gate_w, ple_proj, ln2_g, ln2_b):
    batch, seq, d = x.shape
    depth = w_in.shape[0]
    t = batch * seq
    width = d
    n_exp = router_w.shape[-1]
    alpha = float((2 * depth) ** 0.25)
    rwkv_cols = 3 * width + DECAY_LORA + ICLR_LORA + GATE_LORA
    xt = x.reshape(t, d)
    row = lambda vec: vec.reshape(1, -1)

    for i in range(depth):
        wi = w_in[i]
        o = 2 * width
        w_za = wi[:, :o].astype(BF16)
        w_rkv = wi[:, o:o + 3 * width]
        o2 = o + 3 * width
        w_zw = jnp.pad(wi[:, o2:o2 + DECAY_LORA], ((0, 0), (0, LANES - DECAY_LORA)))
        o3 = o2 + DECAY_LORA
        w_zaa = jnp.pad(wi[:, o3:o3 + ICLR_LORA], ((0, 0), (0, LANES - ICLR_LORA)))
        o4 = o3 + ICLR_LORA
        w_zg = wi[:, o4:o4 + GATE_LORA]
        w_proj = jnp.concatenate([w_rkv, w_zw, w_zaa, w_zg], axis=1).astype(BF16)
        w_gates = wi[:, o + rwkv_cols:].astype(BF16)
        ms = mu_shift[i]
        mu = jnp.concatenate([
            ms[:3 * width],
            jnp.pad(ms[3 * width:3 * width + DECAY_LORA], (0, LANES - DECAY_LORA)),
            jnp.pad(ms[3 * width + DECAY_LORA:3 * width + DECAY_LORA + ICLR_LORA],
                    (0, LANES - ICLR_LORA)),
            ms[3 * width + DECAY_LORA + ICLR_LORA:]]).reshape(1, -1)
        w2p = _pad_rows(w2[i], LANES).astype(BF16)
        a2p = _pad_rows(a2[i], LANES).astype(BF16)
        bias_full = jnp.repeat(sgu_b[i].T, width // SGU_GROUPS, axis=1)

        xb = xt.astype(BF16)
        za = _mm_act(xb, w_za, "gelu", BF16, 1024, 512)
        proj = _mm_act(xb, w_proj, None, F32, 1024, 512)
        gates = _mm_act(xb, w_gates, "sigmoid", BF16, 1024, 512)
        out_a = _sgu(za, row(sgu_ln_g[i]), row(sgu_ln_b[i]), sgu_w[i], bias_full)
        out_b = _rwkv_chunks(proj, mu, row(w0[i]), w2p, row(a0[i]), a2p, g2[i].astype(BF16),
                             row(k_k[i]), row(k_a[i]), row(r_k[i]),
                             row(lnx_g[i]), row(lnx_b[i]), batch, seq)
        x1, x1_slabs = _mix_ln(out_a, out_b, w_out[i, :width].astype(BF16), w_out[i, width:].astype(BF16),
                     gates, xt, row(ln1_g[i]), row(ln1_b[i]), alpha)

        idx_t, wts_t = _router(x1, router_w[i].T.astype(BF16), router_bias[i].reshape(-1, 1))
        order, table, nused = _dispatch(idx_t, n_exp)
        y_assign = _experts(x1_slabs, order, table, nused, exp_gate, exp_up, exp_down, i)
        xt = _final(y_assign, wts_t.T.reshape(-1), x1, p[i].reshape(t, -1),
                    sh_gate[i].astype(BF16), sh_up[i].astype(BF16), sh_down[i].astype(BF16),
                    ple_gate_w[i].astype(BF16), ple_proj[i].astype(BF16),
                    row(ln2_g[i]), row(ln2_b[i]), alpha)
    return xt.reshape(batch, seq, d)
```

```python
import functools
import math

import jax
import jax.numpy as jnp
from jax import lax
from jax.experimental import pallas as pl
from jax.experimental.pallas import tpu as pltpu

F32 = jnp.float32
BF16 = jnp.bfloat16

LANES = 128
VMEM_LIMIT = 56 * 1024 * 1024

SGU_CHUNK = 128
SGU_GROUPS = 16
HEAD = 64
RWKV_CHUNK = 64
HEAD_GROUP_LANES = 256
DECAY_LORA = 96
ICLR_LORA = 96
GATE_LORA = 256
LNX_EPS = 64e-5
LN_EPS = 1e-5
TOP_K = 8
N_GROUPS = 8
TOPK_GROUPS = 4
ROUTED_SCALE = 2.5
EXPERT_BLOCK = 128
INV_SQRT2 = 1.0 / math.sqrt(2.0)


def _cparams(sem, vmem=VMEM_LIMIT):
    return pltpu.CompilerParams(dimension_semantics=sem, vmem_limit_bytes=vmem)


def _dot(a, b):
    return jnp.dot(a, b, preferred_element_type=F32)


def _dot_nt(a, b):
    return lax.dot_general(a, b, (((1,), (1,)), ((), ())), preferred_element_type=F32)


def _layer_norm(z, g, b, eps):
    mu = jnp.mean(z, axis=-1, keepdims=True)
    zc = z - mu
    var = jnp.mean(zc * zc, axis=-1, keepdims=True)
    return zc * lax.rsqrt(var + eps) * g + b


def _rows_to_slabs(slab_ref, x, lead=()):
    rows, d = x.shape
    n = d // LANES
    for j in range(n):
        slab_ref[lead + (pl.ds(j, rows, stride=n), slice(None))] = x[:, j * LANES:(j + 1) * LANES]


def _slabs_to_rows(slab_ref, rows, lead=()):
    n = slab_ref.shape[-2] // rows
    return [slab_ref[lead + (pl.ds(j, rows, stride=n), slice(None))] for j in range(n)]


U32 = jnp.uint32
HI_MASK = 0xFFFF0000


def _pack_pair(lo, hi):
    lo_w = lax.bitcast_convert_type(lo.astype(BF16).astype(F32), U32)
    hi_w = lax.bitcast_convert_type(hi.astype(BF16).astype(F32), U32)
    return (hi_w & U32(HI_MASK)) | (lo_w >> 16)


def _unpack_pair(w):
    lo = lax.bitcast_convert_type(w << 16, F32)
    hi = lax.bitcast_convert_type(w & U32(HI_MASK), F32)
    return lo, hi


def _rows_to_packed_slabs(slab_ref, x, lead=()):
    rows, d = x.shape
    half = d // LANES // 2
    for j in range(half):
        slab_ref[lead + (pl.ds(j, rows, stride=half), slice(None))] = _pack_pair(
            x[:, j * LANES:(j + 1) * LANES], x[:, (j + half) * LANES:(j + half + 1) * LANES])


def _packed_slabs_to_rows(slab_ref, rows, lead=()):
    half = slab_ref.shape[-2] // rows
    pairs = [_unpack_pair(slab_ref[lead + (pl.ds(j, rows, stride=half), slice(None))])
             for j in range(half)]
    return [p[0] for p in pairs] + [p[1] for p in pairs]


def _mm_act_kernel(x_ref, w_ref, o_ref, *, act):
    acc = _dot(x_ref[...], w_ref[...])
    if act == "gelu":
        acc = 0.5 * acc * (1.0 + lax.erf(acc * INV_SQRT2))
    elif act == "sigmoid":
        acc = jax.nn.sigmoid(acc)
    o_ref[...] = acc.astype(o_ref.dtype)


def _mm_act(x, w, act, out_dtype, tm, tn):
    m, k = x.shape
    n = w.shape[1]
    tm = min(tm, m)
    return pl.pallas_call(
        functools.partial(_mm_act_kernel, act=act),
        grid=(m // tm, n // tn),
        in_specs=[pl.BlockSpec((tm, k), lambda i, j: (i, 0)),
                  pl.BlockSpec((k, tn), lambda i, j: (0, j))],
        out_specs=pl.BlockSpec((tm, tn), lambda i, j: (i, j)),
        out_shape=jax.ShapeDtypeStruct((m, n), out_dtype),
        compiler_params=_cparams(("parallel", "parallel")),
        name="in_proj_" + str(act),
    )(x, w)


def _sgu_kernel(u_ref, v_ref, g_ref, b_ref, w_ref, bias_ref, o_ref):
    v = v_ref[...].astype(F32)
    vn = _layer_norm(v, g_ref[...], b_ref[...], LN_EPS).astype(BF16)
    ri = lax.broadcasted_iota(jnp.int32, (SGU_CHUNK, SGU_CHUNK), 0)
    ci = lax.broadcasted_iota(jnp.int32, (SGU_CHUNK, SGU_CHUNK), 1)
    causal = ri >= ci
    for g in range(SGU_GROUPS):
        sl = slice(g * LANES, (g + 1) * LANES)
        wg = jnp.where(causal, w_ref[g], 0.0).astype(BF16)
        vm = _dot(wg, vn[:, sl]) + bias_ref[:, sl]
        o_ref[:, sl] = (u_ref[:, sl].astype(F32) * vm).astype(o_ref.dtype)


def _sgu(za, ln_g, ln_b, w_s, bias_full):
    t = za.shape[0]
    width = za.shape[1] // 2
    return pl.pallas_call(
        _sgu_kernel,
        grid=(t // SGU_CHUNK,),
        in_specs=[pl.BlockSpec((SGU_CHUNK, width), lambda c: (c, 0)),
                  pl.BlockSpec((SGU_CHUNK, width), lambda c: (c, 1)),
                  pl.BlockSpec((1, width), lambda c: (0, 0)),
                  pl.BlockSpec((1, width), lambda c: (0, 0)),
                  pl.BlockSpec((SGU_GROUPS, SGU_CHUNK, SGU_CHUNK), lambda c: (0, 0, 0)),
                  pl.BlockSpec((SGU_CHUNK, width), lambda c: (0, 0))],
        out_specs=pl.BlockSpec((SGU_CHUNK, width), lambda c: (c, 0)),
        out_shape=jax.ShapeDtypeStruct((t, width), BF16),
        compiler_params=_cparams(("parallel",)),
        name="sgu",
    )(za, za, ln_g, ln_b, w_s, bias_full)


def _rwkv_token_inputs(p_ref, prev_ref, mu_ref, w0_ref, w2_ref, a0_ref, a2_ref, g2_ref,
                       first, width):
    tm = p_ref.shape[0]
    row0 = lax.broadcasted_iota(jnp.int32, (tm, 1), 0) == 0

    def shifted(lo, hi):
        p = p_ref[:, lo:hi]
        last = jnp.where(first, 0.0, prev_ref[7:8, lo:hi])
        prev = jnp.where(row0, last, pltpu.roll(p, 1, axis=0))
        return p + (prev - p) * mu_ref[:, lo:hi]

    r = shifted(0, width)
    k = shifted(width, 2 * width)
    v = shifted(2 * width, 3 * width)
    o = 3 * width
    zw = shifted(o, o + LANES)
    za = shifted(o + LANES, o + 2 * LANES)
    zg = shifted(o + 2 * LANES, o + 2 * LANES + GATE_LORA)
    dw = w0_ref[...] + _dot(jnp.tanh(zw).astype(BF16), w2_ref[...])
    w_log = -(jnp.maximum(-dw, 0.0) + jnp.log(1.0 + jnp.exp(-jnp.abs(dw)))) - 0.5
    lw = -jnp.exp(w_log)
    a = jax.nn.sigmoid(a0_ref[...] + _dot(za.astype(BF16), a2_ref[...]))
    g = _dot(jax.nn.sigmoid(zg).astype(BF16), g2_ref[...])
    return r, k, v, lw, a, g


def _rwkv_chunk_kernel(p_ref, prev_ref, mu_ref, w0_ref, w2_ref, a0_ref, a2_ref, g2_ref,
                       kk_ref, ka_ref, rk_ref, lg_ref, lb_ref,
                       o_ref, h_scr):
    c = RWKV_CHUNK
    w = o_ref.shape[1]
    gw = HEAD_GROUP_LANES
    ng = w // gw
    hpg = gw // HEAD

    @pl.when(pl.program_id(1) == 0)
    def _():
        h_scr[...] = jnp.zeros_like(h_scr)

    bi = lax.broadcasted_iota(jnp.int32, (gw, gw), 0) // HEAD
    bj = lax.broadcasted_iota(jnp.int32, (gw, gw), 1) // HEAD
    block_mask = bi == bj
    block_ones = block_mask.astype(BF16)

    def bd(y):
        return jnp.where(block_mask, jnp.concatenate([y] * hpg, axis=0), jnp.zeros((), y.dtype))

    ti = lax.broadcasted_iota(jnp.int32, (c, gw), 0)
    tj = lax.broadcasted_iota(jnp.int32, (c, gw), 1) % HEAD
    strict = ti > tj
    incl = ti >= tj
    eye = ti == tj
    eye_f = eye.astype(F32)
    eye_b = eye.astype(BF16)
    groups = range(ng)
    gsl = [slice(g * gw, (g + 1) * gw) for g in groups]

    def head_sums(*xs):
        parts = []
        for x in xs:
            hi = x.astype(BF16)
            parts += [hi, (x - hi.astype(F32)).astype(BF16)]
        stacked = jnp.concatenate(parts, axis=0)
        sums = jnp.concatenate([_dot(stacked[:, s], block_ones) for s in gsl], axis=1)
        return [sums[2 * i * c:(2 * i + 1) * c] + sums[(2 * i + 1) * c:(2 * i + 2) * c]
                for i in range(len(xs))]

    r, k, v, lw, a, gate = _rwkv_token_inputs(
        p_ref, prev_ref, mu_ref, w0_ref, w2_ref, a0_ref, a2_ref, g2_ref,
        pl.program_id(1) == 0, w)

    t0 = lax.broadcasted_iota(jnp.int32, (c, c), 0)
    t1 = lax.broadcasted_iota(jnp.int32, (c, c), 1)
    tri = (t0 >= t1).astype(BF16)
    lw_hi = lw.astype(BF16)
    lw_rest = lw - lw_hi.astype(F32)
    lw_mid = lw_rest.astype(BF16)
    lw_lo = (lw_rest - lw_mid.astype(F32)).astype(BF16)
    gcum = _dot(jnp.concatenate([tri] * 3, axis=1),
                jnp.concatenate([lw_hi, lw_mid, lw_lo], axis=0))
    g_last = gcum[c - 1:c, :]
    e_g = jnp.exp(gcum)
    e_ng = jnp.exp(-gcum)
    e_gp = jnp.exp(gcum - lw)
    e_h = jnp.exp(g_last - gcum)
    e_last = jnp.exp(g_last)

    kk = k * kk_ref[...]
    kmod = k * (1.0 + (a - 1.0) * ka_ref[...])
    kk_sq, rk_sum = head_sums(kk * kk, r * kmod * rk_ref[...])
    kk = kk / jnp.maximum(jnp.sqrt(kk_sq), 1e-12)
    kb = kk * a
    bonus = rk_sum * v

    at_all = (-kk * e_gp).astype(BF16)
    bt_all = (kb * e_ng).astype(BF16)
    kt_all = (kmod * e_ng).astype(BF16)
    rt_all = r * e_g
    bh_all = (kb * e_h).astype(BF16)
    kh_all = (kmod * e_h).astype(BF16)
    v_all = v.astype(BF16)

    at = [at_all[:, s] for s in gsl]
    rt = [rt_all[:, s] for s in gsl]
    rtb = [x.astype(BF16) for x in rt]
    bt_d = [bd(bt_all[:, s]) for s in gsl]
    kt_d = [bd(kt_all[:, s]) for s in gsl]
    vb_d = [bd(v_all[:, s]) for s in gsl]
    at_rt = [jnp.concatenate([at[g], rtb[g]], axis=0) for g in groups]
    ab_rb = [_dot_nt(at_rt[g], bt_d[g]) for g in groups]
    ak_rk = [_dot_nt(at_rt[g], kt_d[g]) for g in groups]
    a_ab = [jnp.where(strict, x[:c], 0.0) for x in ab_rb]
    m_rb = [jnp.where(incl, x[c:], 0.0).astype(BF16) for x in ab_rb]
    a_ak = [jnp.where(strict, x[:c], 0.0).astype(BF16) for x in ak_rk]
    m_rk = [jnp.where(incl, x[c:], 0.0).astype(BF16) for x in ak_rk]
    bh_t = [_dot_nt(eye_b, bd(bh_all[:, s])).astype(BF16) for s in gsl]
    kh_t = [_dot_nt(eye_b, bd(kh_all[:, s])).astype(BF16) for s in gsl]
    v_prod = [_dot(jnp.concatenate([a_ak[g], m_rk[g], kh_t[g]], axis=0), vb_d[g]) for g in groups]
    av = [x[:c].astype(BF16) for x in v_prod]
    tinv = [eye_f + x for x in a_ab]
    pw = a_ab
    for _ in range(5):
        pwb = [x.astype(BF16) for x in pw]
        pw = [_dot(pwb[g], bd(pwb[g])) for g in groups]
        tinv = [tinv[g] + _dot(tinv[g].astype(BF16), bd(pw[g].astype(BF16))) for g in groups]
    tb = [x.astype(BF16) for x in tinv]
    w1 = [_dot(tb[g], bd(at[g])).astype(BF16) for g in groups]
    x2 = [_dot(tb[g], bd(av[g])).astype(BF16) for g in groups]
    rb_bh = [jnp.concatenate([m_rb[g], bh_t[g]], axis=0) for g in groups]
    w1_prod = [_dot(rb_bh[g], bd(w1[g])) for g in groups]
    x2_prod = [_dot(rb_bh[g], bd(x2[g])) for g in groups]
    q1 = [(rt[g] + w1_prod[g][:c]).astype(BF16) for g in groups]
    gmat = [(jnp.where(eye, e_last[:, gsl[g]], 0.0) + w1_prod[g][c:]).astype(BF16)
            for g in groups]
    o2 = [x2_prod[g][:c] + v_prod[g][c:2 * c] for g in groups]
    dmat = [x2_prod[g][c:] + v_prod[g][2 * c:] for g in groups]
    h_prod = [_dot(jnp.concatenate([q1[g], gmat[g]], axis=0), bd(h_scr[g].astype(BF16)))
              for g in groups]
    y = jnp.concatenate([h_prod[g][:c] + o2[g] for g in groups], axis=1)
    for g in groups:
        h_scr[g] = h_prod[g][c:] + dmat[g]

    ym = head_sums(y)[0] * (1.0 / HEAD)
    yc = y - ym
    yv = head_sums(yc * yc)[0] * (1.0 / HEAD)
    yn = yc * lax.rsqrt(yv + LNX_EPS) * lg_ref[...] + lb_ref[...]
    o_ref[...] = ((yn + bonus) * gate).astype(o_ref.dtype)


def _rwkv_chunks(proj, mu, w0, w2p, a0, a2p, g2, k_k, k_a, r_k, lnx_g, lnx_b, batch, seq):
    t, pc = proj.shape
    width = w0.shape[1]
    nc = seq // RWKV_CHUNK
    const = lambda b, c: (0, 0)
    par = pl.BlockSpec((1, width), const)
    return pl.pallas_call(
        _rwkv_chunk_kernel,
        grid=(batch, nc),
        in_specs=[pl.BlockSpec((RWKV_CHUNK, pc), lambda b, c: (b * nc + c, 0)),
                  pl.BlockSpec((8, pc), lambda b, c: (
                      jnp.maximum((b * nc + c) * (RWKV_CHUNK // 8) - 1, 0), 0)),
                  pl.BlockSpec((1, pc), const),
                  par, pl.BlockSpec((LANES, width), const),
                  par, pl.BlockSpec((LANES, width), const),
                  pl.BlockSpec((GATE_LORA, width), const)] + [par] * 5,
        out_specs=pl.BlockSpec((RWKV_CHUNK, width), lambda b, c: (b * nc + c, 0)),
        out_shape=jax.ShapeDtypeStruct((t, width), BF16),
        scratch_shapes=[pltpu.VMEM((width // HEAD_GROUP_LANES, HEAD, HEAD_GROUP_LANES), F32)],
        compiler_params=_cparams(("parallel", "arbitrary")),
        name="rwkv_chunks",
    )(proj, proj, mu, w0, w2p, a0, a2p, g2, k_k, k_a, r_k, lnx_g, lnx_b)


def _mix_ln_kernel(oa_ref, ob_ref, wa_ref, wb_ref, ga_ref, gb_ref, x_ref, g_ref, b_ref,
                   x1_ref, slab_ref, *, alpha):
    tm, d = x_ref.shape
    ma = _dot(oa_ref[...], wa_ref[...])
    mb = _dot(ob_ref[...], wb_ref[...])
    mix = ga_ref[...].astype(F32) * ma + gb_ref[...].astype(F32) * mb
    z = alpha * x_ref[...] + mix
    x1 = _layer_norm(z, g_ref[...], b_ref[...], LN_EPS)
    x1_ref[...] = x1
    _rows_to_packed_slabs(slab_ref, x1)


def _mix_ln(out_a, out_b, wa, wb, gates, x, ln_g, ln_b, alpha, tm=256):
    t, d = x.shape
    row = lambda i: (i, 0)
    const = lambda i: (0, 0)
    return pl.pallas_call(
        functools.partial(_mix_ln_kernel, alpha=alpha),
        grid=(t // tm,),
        in_specs=[pl.BlockSpec((tm, d), row), pl.BlockSpec((tm, d), row),
                  pl.BlockSpec((d, d), const, pipeline_mode=pl.Buffered(1)),
                  pl.BlockSpec((d, d), const, pipeline_mode=pl.Buffered(1)),
                  pl.BlockSpec((tm, d), row), pl.BlockSpec((tm, d), lambda i: (i, 1)),
                  pl.BlockSpec((tm, d), row),
                  pl.BlockSpec((1, d), const), pl.BlockSpec((1, d), const)],
        out_specs=[pl.BlockSpec((tm, d), row), pl.BlockSpec((tm * (d // LANES // 2), LANES), row)],
        out_shape=[jax.ShapeDtypeStruct((t, d), F32),
                   jax.ShapeDtypeStruct((t * (d // LANES // 2), LANES), U32)],
        compiler_params=_cparams(("parallel",)),
        name="mix_ln",
    )(out_a, out_b, wa, wb, gates, gates, x, ln_g, ln_b)


def _router_kernel(x_ref, wr_ref, bias_ref, idx_ref, wt_ref):
    n_exp = wr_ref.shape[0]
    tm = x_ref.shape[0]
    gsz = n_exp // N_GROUPS
    scores = jax.nn.sigmoid(_dot_nt(wr_ref[...], x_ref[...].astype(BF16)))
    biased = scores + bias_ref[...]
    neg = -jnp.inf

    grp_scores = []
    rows_g = lax.broadcasted_iota(jnp.int32, (gsz, tm), 0)
    for g in range(N_GROUPS):
        blk = biased[g * gsz:(g + 1) * gsz]
        m1 = jnp.max(blk, axis=0, keepdims=True)
        i1 = jnp.min(jnp.where(blk == m1, rows_g, gsz), axis=0, keepdims=True)
        m2 = jnp.max(jnp.where(rows_g == i1, neg, blk), axis=0, keepdims=True)
        grp_scores.append(m1 + m2)
    masked = []
    for g in range(N_GROUPS):
        rank = jnp.zeros((1, tm), jnp.int32)
        for g2 in range(N_GROUPS):
            if g2 == g:
                continue
            ahead = grp_scores[g2] > grp_scores[g]
            if g2 < g:
                ahead = ahead | (grp_scores[g2] == grp_scores[g])
            rank = rank + ahead.astype(jnp.int32)
        blk = biased[g * gsz:(g + 1) * gsz]
        masked.append(jnp.where(rank < TOPK_GROUPS, blk, neg))
    cand = jnp.concatenate(masked, axis=0)

    rows = lax.broadcasted_iota(jnp.int32, (n_exp, tm), 0)
    ids, wts = [], []
    for _ in range(TOP_K):
        m = jnp.max(cand, axis=0, keepdims=True)
        i = jnp.min(jnp.where(cand == m, rows, n_exp), axis=0, keepdims=True)
        hit = rows == i
        ids.append(i)
        wts.append(jnp.sum(jnp.where(hit, scores, 0.0), axis=0, keepdims=True))
        cand = jnp.where(hit, neg, cand)
    wt = jnp.concatenate(wts, axis=0)
    wt = wt / jnp.sum(wt, axis=0, keepdims=True) * ROUTED_SCALE
    idx_ref[...] = jnp.concatenate(ids, axis=0)
    wt_ref[...] = wt


def _router(x1, wr_t, bias_col, tm=512):
    t, d = x1.shape
    n_exp = wr_t.shape[0]
    return pl.pallas_call(
        _router_kernel,
        grid=(t // tm,),
        in_specs=[pl.BlockSpec((tm, d), lambda i: (i, 0)),
                  pl.BlockSpec((n_exp, d), lambda i: (0, 0)),
                  pl.BlockSpec((n_exp, 1), lambda i: (0, 0))],
        out_specs=[pl.BlockSpec((TOP_K, tm), lambda i: (0, i)),
                   pl.BlockSpec((TOP_K, tm), lambda i: (0, i))],
        out_shape=[jax.ShapeDtypeStruct((TOP_K, t), jnp.int32),
                   jax.ShapeDtypeStruct((TOP_K, t), F32)],
        compiler_params=_cparams(("parallel",)),
        name="router",
    )(x1, wr_t, bias_col)


BLK_FIELDS = 8
M_EXPERT, M_BASE, M_VALID, M_FIRST, M_WSLOT, M_NEXT = range(6)


def _experts_kernel(order_ref, meta_ref, nused_ref, x_hbm, wg_hbm, wu_hbm, wd_hbm, y_hbm,
                    xbuf, ybuf, wg_f, wu_f, wd_f, wg_b, wu_b, wd_b, gsem, ssem, wsem,
                    *, layer, n_exp):
    i = pl.program_id(0)
    nused = nused_ref[0]
    n_assign = order_ref.shape[0]
    spr = xbuf.shape[1] // EXPERT_BLOCK
    slot = i % 2

    def meta(b, j):
        return meta_ref[b * BLK_FIELDS + j]

    def assignment(b, r):
        return order_ref[jnp.minimum(meta(b, M_BASE) + r, n_assign - 1)]

    def slab(row):
        return pl.ds(pl.multiple_of(row * spr, spr), spr)

    def gather_copy(tok, s, r):
        return pltpu.make_async_copy(x_hbm.at[slab(tok)], xbuf.at[s, slab(r)], gsem.at[s])

    def scatter_copy(dst, s, r):
        return pltpu.make_async_copy(ybuf.at[s, slab(r)], y_hbm.at[slab(dst)], ssem.at[s])

    def weight_copies(e, ws):
        return (pltpu.make_async_copy(wg_hbm.at[layer, e], wg_f.at[ws], wsem.at[ws]),
                pltpu.make_async_copy(wu_hbm.at[layer, e], wu_f.at[ws], wsem.at[ws]),
                pltpu.make_async_copy(wd_hbm.at[layer, e], wd_f.at[ws], wsem.at[ws]))

    def start_gather(b, s):
        for r in range(EXPERT_BLOCK):
            tok = lax.shift_right_logical(assignment(b, r), TOP_K.bit_length() - 1)
            gather_copy(tok, s, r).start()

    def wait_scatter(s):
        for r in range(EXPERT_BLOCK):
            scatter_copy(0, s, r).wait()

    def start_scatter(b, nvalid, s):
        for r in range(EXPERT_BLOCK):
            dst = jnp.where(r < nvalid, assignment(b, r), n_assign + s * EXPERT_BLOCK + r)
            scatter_copy(dst, s, r).start()

    def wait_gather(s):
        for r in range(EXPERT_BLOCK):
            gather_copy(0, s, r).wait()

    @pl.when(i == 0)
    def _():
        ybuf[...] = jnp.zeros(ybuf.shape, ybuf.dtype)
        for s in range(2):
            dump = pltpu.make_async_copy(
                ybuf.at[0],
                y_hbm.at[pl.ds((n_assign + s * EXPERT_BLOCK) * spr, EXPERT_BLOCK * spr)],
                ssem.at[0])
            dump.start()
            dump.wait()

    @pl.when((i == 0) & (nused > 0))
    def _():
        start_gather(0, 0)
        for cp in weight_copies(meta(0, M_EXPERT), 0):
            cp.start(priority=1)

    @pl.when(i < nused)
    def _():
        ws = meta(i, M_WSLOT)

        @pl.when(meta(i, M_FIRST) == 1)
        def _():
            for cp in weight_copies(0, ws):
                cp.wait()
            nxt = meta(i, M_NEXT)

            @pl.when(nxt < n_exp)
            def _():
                for cp in weight_copies(nxt, 1 - ws):
                    cp.start(priority=1)

            wg_b[...] = wg_f[ws].astype(BF16)
            wu_b[...] = wu_f[ws].astype(BF16)
            wd_b[...] = wd_f[ws].astype(BF16)

        @pl.when(i >= 1)
        def _():
            wait_scatter(slot)

        wait_gather(slot)
        start_gather(jnp.minimum(i + 1, nused - 1), 1 - slot)
        prev_valid = jnp.where(i >= 1, meta(jnp.maximum(i - 1, 0), M_VALID), 0)
        start_scatter(jnp.maximum(i - 1, 0), prev_valid, 1 - slot)
        xb = jnp.concatenate(_packed_slabs_to_rows(xbuf, EXPERT_BLOCK, (slot,)),
                             axis=1).astype(BF16)
        hg = _dot(xb, wg_b[...])
        hu = _dot(xb, wu_b[...])
        hid = (hg * jax.nn.sigmoid(hg) * hu).astype(BF16)
        _rows_to_packed_slabs(ybuf, _dot(hid, wd_b[...]), (slot,))

        @pl.when(i == nused - 1)
        def _():
            start_scatter(i, meta(i, M_VALID), slot)
            wait_scatter(1 - slot)
            wait_scatter(slot)
            wait_gather(1 - slot)


def _experts(x1_slabs, order, meta, nused, w_gate, w_up, w_down, layer):
    d = w_gate.shape[2]
    spr = d // LANES // 2
    n_blocks = meta.shape[0] // BLK_FIELDS
    n_exp, ff = w_gate.shape[1], w_gate.shape[-1]
    n_assign = order.shape[0]
    any_spec = pl.BlockSpec(memory_space=pl.ANY)
    grid_spec = pltpu.PrefetchScalarGridSpec(
        num_scalar_prefetch=3,
        grid=(n_blocks,),
        in_specs=[any_spec] * 4,
        out_specs=any_spec,
        scratch_shapes=[pltpu.VMEM((2, EXPERT_BLOCK * spr, LANES), U32),
                        pltpu.VMEM((2, EXPERT_BLOCK * spr, LANES), U32),
                        pltpu.VMEM((2, d, ff), F32), pltpu.VMEM((2, d, ff), F32),
                        pltpu.VMEM((2, ff, d), F32),
                        pltpu.VMEM((d, ff), BF16), pltpu.VMEM((d, ff), BF16),
                        pltpu.VMEM((ff, d), BF16),
                        pltpu.SemaphoreType.DMA((2,)), pltpu.SemaphoreType.DMA((2,)),
                        pltpu.SemaphoreType.DMA((2,))],
    )
    return pl.pallas_call(
        functools.partial(_experts_kernel, layer=layer, n_exp=n_exp),
        grid_spec=grid_spec,
        out_shape=jax.ShapeDtypeStruct(((n_assign + 2 * EXPERT_BLOCK) * spr, LANES), U32),
        compiler_params=_cparams(("arbitrary",)),
        name="experts",
    )(order, meta, nused, x1_slabs, w_gate, w_up, w_down)


def _final_kernel(w_ref, y_ref, x_ref, p_ref, sg_ref, su_ref, sd_ref, pg_ref, pp_ref,
                  g_ref, b_ref, o_ref, routed_scr, *, alpha):
    tm = x_ref.shape[0]
    spr = y_ref.shape[1] // TOP_K
    base = pl.program_id(0) * (tm * TOP_K)

    def combine(t, carry):
        acc_lo = jnp.zeros((spr, LANES), F32)
        acc_hi = jnp.zeros((spr, LANES), F32)
        for k in range(TOP_K):
            w = w_ref[base + t * TOP_K + k]
            lo, hi = _unpack_pair(y_ref[t, k * spr:(k + 1) * spr, :])
            acc_lo = acc_lo + w * lo
            acc_hi = acc_hi + w * hi
        row0 = pl.multiple_of(t * (2 * spr), 2 * spr)
        routed_scr[pl.ds(row0, spr), :] = acc_lo
        routed_scr[pl.ds(row0 + spr, spr), :] = acc_hi
        return carry

    lax.fori_loop(0, tm, combine, 0, unroll=8)
    routed = jnp.concatenate(_slabs_to_rows(routed_scr, tm), axis=1)
    x = x_ref[...]
    xb = x.astype(BF16)
    hg = _dot(xb, sg_ref[...])
    hu = _dot(xb, su_ref[...])
    shared = _dot((hg * jax.nn.sigmoid(hg) * hu).astype(BF16), sd_ref[...])
    ple = jax.nn.sigmoid(_dot(xb, pg_ref[...])) * _dot(p_ref[...].astype(BF16), pp_ref[...])
    z = alpha * x + routed + shared + ple
    o_ref[...] = _layer_norm(z, g_ref[...], b_ref[...], LN_EPS)


def _final(y_assign, wts_flat, x1, p, sg, su, sd, pg, pp, ln_g, ln_b, alpha, tm=256):
    t, d = x1.shape
    ff = sg.shape[1]
    pd = p.shape[1]
    spr = d // LANES // 2
    y3 = y_assign.reshape(-1, TOP_K * spr, LANES)
    row = lambda i, w: (i, 0)
    const = lambda i, w: (0, 0)
    one = pl.Buffered(1)
    grid_spec = pltpu.PrefetchScalarGridSpec(
        num_scalar_prefetch=1,
        grid=(t // tm,),
        in_specs=[pl.BlockSpec((tm, TOP_K * spr, LANES), lambda i, w: (i, 0, 0)),
                  pl.BlockSpec((tm, d), row),
                  pl.BlockSpec((tm, pd), row),
                  pl.BlockSpec((d, ff), const, pipeline_mode=one),
                  pl.BlockSpec((d, ff), const, pipeline_mode=one),
                  pl.BlockSpec((ff, d), const, pipeline_mode=one),
                  pl.BlockSpec((d, d), const, pipeline_mode=one),
                  pl.BlockSpec((pd, d), const, pipeline_mode=one),
                  pl.BlockSpec((1, d), const), pl.BlockSpec((1, d), const)],
        out_specs=pl.BlockSpec((tm, d), row),
        scratch_shapes=[pltpu.VMEM((tm * 2 * spr, LANES), F32)],
    )
    return pl.pallas_call(
        functools.partial(_final_kernel, alpha=alpha),
        grid_spec=grid_spec,
        out_shape=jax.ShapeDtypeStruct((t, d), F32),
        compiler_params=_cparams(("parallel",)),
        name="final",
    )(wts_flat, y3, x1, p, sg, su, sd, pg, pp, ln_g, ln_b)


def _dispatch(idx_t, n_exp):
    k, t = idx_t.shape
    n_assign = t * k
    n_blocks = (n_assign + n_exp * (EXPERT_BLOCK - 1) + EXPERT_BLOCK - 1) // EXPERT_BLOCK
    i32 = jnp.int32
    flat_e = idx_t.T.reshape(-1)
    _, order = lax.sort_key_val(flat_e, jnp.arange(n_assign, dtype=i32))
    e_ids = jnp.arange(n_exp, dtype=i32)
    counts = jnp.sum((flat_e[None, :] == e_ids[:, None]).astype(i32), axis=1)
    end = jnp.cumsum(counts).astype(i32)
    start = end - counts
    nblk = (counts + EXPERT_BLOCK - 1) // EXPERT_BLOCK
    blk_end = jnp.cumsum(nblk).astype(i32)
    blk_start = blk_end - nblk
    nused = blk_end[-1:]
    used = counts > 0
    wslot = (jnp.cumsum(used.astype(i32)) - 1) % 2
    cand = jnp.where(used, e_ids, n_exp)
    nxt = jnp.concatenate([lax.cummin(cand, axis=0, reverse=True)[1:], jnp.full((1,), n_exp, i32)])
    b = jnp.arange(n_blocks, dtype=i32)
    e_of_b = jnp.minimum(jnp.sum((blk_end[None, :] <= b[:, None]).astype(i32), axis=1), n_exp - 1)
    onehot = e_of_b[:, None] == e_ids[None, :]
    pick = lambda v: jnp.sum(jnp.where(onehot, v[None, :].astype(i32), 0), axis=1)
    base = pick(start) + (b - pick(blk_start)) * EXPERT_BLOCK
    nvalid = jnp.clip(pick(end) - base, 0, EXPERT_BLOCK)
    first = (b == pick(blk_start)).astype(i32)
    zeros = jnp.zeros_like(b)
    table = jnp.stack([e_of_b, base, nvalid, first, pick(wslot), pick(nxt), zeros, zeros],
                      axis=1).reshape(-1).astype(i32)
    return order, table, nused.astype(i32)


def _pad_rows(w, rows):
    return jnp.pad(w, ((0, rows - w.shape[0]), (0, 0)))


def kernel(x, p, w_in, mu_shift, w0, w2, a0, a2, g2, k_k, k_a, r_k, lnx_g, lnx_b, sgu_ln_g, sgu_ln_b, sgu_w, sgu_b, w_out, ln1_g, ln1_b, router_w, router_bias, exp_gate, exp_up, exp_down, sh_gate, sh_up, sh_down, ple_gate_w, ple_proj, ln2_g, ln2_b):
    batch, seq, d = x.shape
    depth = w_in.shape[0]
    t = batch * seq
    width = d
    n_exp = router_w.shape[-1]
    alpha = float((2 * depth) ** 0.25)
    rwkv_cols = 3 * width + DECAY_LORA + ICLR_LORA + GATE_LORA
    xt = x.reshape(t, d)
    row = lambda vec: vec.reshape(1, -1)

    for i in range(depth):
        wi = w_in[i]
        o = 2 * width
        w_za = wi[:, :o].astype(BF16)
        w_rkv = wi[:, o:o + 3 * width]
        o2 = o + 3 * width
        w_zw = jnp.pad(wi[:, o2:o2 + DECAY_LORA], ((0, 0), (0, LANES - DECAY_LORA)))
        o3 = o2 + DECAY_LORA
        w_zaa = jnp.pad(wi[:, o3:o3 + ICLR_LORA], ((0, 0), (0, LANES - ICLR_LORA)))
        o4 = o3 + ICLR_LORA
        w_zg = wi[:, o4:o4 + GATE_LORA]
        w_proj = jnp.concatenate([w_rkv, w_zw, w_zaa, w_zg], axis=1).astype(BF16)
        w_gates = wi[:, o + rwkv_cols:].astype(BF16)
        ms = mu_shift[i]
        mu = jnp.concatenate([
            ms[:3 * width],
            jnp.pad(ms[3 * width:3 * width + DECAY_LORA], (0, LANES - DECAY_LORA)),
            jnp.pad(ms[3 * width + DECAY_LORA:3 * width + DECAY_LORA + ICLR_LORA],
                    (0, LANES - ICLR_LORA)),
            ms[3 * width + DECAY_LORA + ICLR_LORA:]]).reshape(1, -1)
        w2p = _pad_rows(w2[i], LANES).astype(BF16)
        a2p = _pad_rows(a2[i], LANES).astype(BF16)
        bias_full = jnp.repeat(sgu_b[i].T, width // SGU_GROUPS, axis=1)

        xb = xt.astype(BF16)
        za = _mm_act(xb, w_za, "gelu", BF16, 1024, 1024)
        proj = _mm_act(xb, w_proj, None, F32, 1024, 512)
        gates = _mm_act(xb, w_gates, "sigmoid", BF16, 1024, 1024)
        out_a = _sgu(za, row(sgu_ln_g[i]), row(sgu_ln_b[i]), sgu_w[i], bias_full)
        out_b = _rwkv_chunks(proj, mu, row(w0[i]), w2p, row(a0[i]), a2p, g2[i].astype(BF16),
                             row(k_k[i]), row(k_a[i]), row(r_k[i]),
                             row(lnx_g[i]), row(lnx_b[i]), batch, seq)
        x1, x1_slabs = _mix_ln(out_a, out_b, w_out[i, :width].astype(BF16), w_out[i, width:].astype(BF16),
                     gates, xt, row(ln1_g[i]), row(ln1_b[i]), alpha)

        idx_t, wts_t = _router(x1, router_w[i].T.astype(BF16), router_bias[i].reshape(-1, 1))
        order, table, nused = _dispatch(idx_t, n_exp)
        y_assign = _experts(x1_slabs, order, table, nused, exp_gate, exp_up, exp_down, i)
        xt = _final(y_assign, wts_t.T.reshape(-1), x1, p[i].reshape(t, -1),
                    sh_gate[i].astype(BF16), sh_up[i].astype(BF16), sh_down[i].astype(BF16),
                    ple_gate_w[i].astype(BF16), ple_proj[i].astype(BF16),
                    row(ln2_g[i]), row(ln2_b[i]), alpha)
    return xt.reshape(batch, seq, d)
```
